```python
import jax, jax.numpy as jnp
from jax import lax
import numpy as np

D_MODEL = 2048
BATCH = 8
SEQ = 2048
DEPTH = 2

N_MIXERS = 2
GRID_W = 64
NORM_EPS = 1e-6

MLSTM_HEADS = 4
MLSTM_V_DIM = D_MODEL // MLSTM_HEADS
MLSTM_QK_DIM = MLSTM_V_DIM // 2
MLSTM_CHUNK = 128
MLSTM_F_BIAS_LO = 3.0
MLSTM_F_BIAS_HI = 6.0
MLSTM_IN = 2 * MLSTM_HEADS * MLSTM_QK_DIM + 2 * MLSTM_HEADS * MLSTM_V_DIM + 4 * MLSTM_HEADS

ATTN_HEAD_DIM = 128
ATTN_Q_HEADS = D_MODEL // ATTN_HEAD_DIM
ATTN_KV_HEADS = 4
ATTN_GROUPS = ATTN_Q_HEADS // ATTN_KV_HEADS
ATTN_Q_BLOCK = 128
ROPE_THETA = 10000.0
ATTN_IN = (ATTN_Q_HEADS + 2 * ATTN_KV_HEADS) * ATTN_HEAD_DIM

PEER_HEADS = 8
PEER_N_KEYS = 128
PEER_EXPERTS = PEER_N_KEYS ** 2
PEER_TOPK = 16
PEER_QUERY_DIM = 256
PEER_HALF = PEER_QUERY_DIM // 2
PEER_TOKEN_BLOCK = 128
PEER_V_SCALE = PEER_HEADS ** -0.5

kernel_name = "hybrid_mlstm_axialgqa_peer_encoder"


def rms_norm(x, g):
    xf = x.astype(jnp.float32)
    y = xf * lax.rsqrt(jnp.mean(xf * xf, axis=-1, keepdims=True) + NORM_EPS)
    return (y * g.astype(jnp.float32)).astype(x.dtype)


def mlstm_chunkwise(q, k, v, ig, fg):
    B, H, S, dk = q.shape
    dv = v.shape[-1]
    L = MLSTM_CHUNK
    nc = S // L

    def chunks(a):
        return jnp.moveaxis(a.reshape(B, H, nc, L, *a.shape[3:]), 2, 0)

    lower = jnp.tril(jnp.ones((L, L), dtype=bool))

    def step(carry, inp):
        C, n, m = carry
        q_, k_, v_, i_, f_ = inp
        b = jnp.cumsum(jax.nn.log_sigmoid(f_), axis=-1)
        log_d = jnp.where(lower, b[..., :, None] - b[..., None, :] + i_[..., None, :], -jnp.inf)
        m_inter = b + m[..., None]
        m_t = jnp.maximum(m_inter, jnp.max(log_d, axis=-1))
        dmat = jnp.exp(log_d - m_t[..., None])
        inter = jnp.exp(m_inter - m_t)
        s = jnp.einsum('bhtd,bhsd->bhts', q_, k_) * dmat
        num = jnp.einsum('bhts,bhsv->bhtv', s, v_) + inter[..., None] * jnp.einsum('bhvd,bhtd->bhtv', C, q_)
        den = jnp.sum(s, axis=-1) + inter * jnp.einsum('bhd,bhtd->bht', n, q_)
        h = num / jnp.maximum(jnp.abs(den), jnp.exp(-m_t))[..., None]
        b_last = b[..., -1]
        log_w = b_last[..., None] - b + i_
        m_new = jnp.maximum(b_last + m, jnp.max(log_w, axis=-1))
        w = jnp.exp(log_w - m_new[..., None])
        decay = jnp.exp(b_last + m - m_new)
        C = decay[..., None, None] * C + jnp.einsum('bhs,bhsv,bhsd->bhvd', w, v_, k_)
        n = decay[..., None] * n + jnp.einsum('bhs,bhsd->bhd', w, k_)
        return (C, n, m_new), h

    init = (jnp.zeros((B, H, dv, dk), jnp.float32),
            jnp.zeros((B, H, dk), jnp.float32),
            jnp.zeros((B, H), jnp.float32))
    _, h = lax.scan(step, init, (chunks(q), chunks(k), chunks(v), chunks(ig), chunks(fg)))
    return jnp.moveaxis(h, 0, 2).reshape(B, H, S, dv)


def mlstm_mixer(xn, w_in, gate_bias, head_norm, w_out):
    B, S, _ = xn.shape
    H = MLSTM_HEADS
    nqk = H * MLSTM_QK_DIM
    nv = H * MLSTM_V_DIM
    proj = xn @ w_in
    q, k, v, o, gates = jnp.split(proj, [nqk, 2 * nqk, 2 * nqk + nv, 2 * nqk + 2 * nv], axis=-1)

    def heads(a, d):
        return a.reshape(B, S, H, d).transpose(0, 2, 1, 3).astype(jnp.float32)

    q = heads(q, MLSTM_QK_DIM) * (MLSTM_QK_DIM ** -0.5)
    k = heads(k, MLSTM_QK_DIM)
    v = heads(v, MLSTM_V_DIM)
    g = (gates.reshape(B, S, 4, H).astype(jnp.float32) + gate_bias.astype(jnp.float32)).transpose(2, 0, 3, 1)
    h_fwd = mlstm_chunkwise(q, k, v, g[0], g[1])
    flip = lambda a: jnp.flip(a, axis=2)
    h_bwd = flip(mlstm_chunkwise(flip(q), flip(k), flip(v), flip(g[2]), flip(g[3])))
    h = h_fwd + h_bwd
    h = h * lax.rsqrt(jnp.mean(h * h, axis=-1, keepdims=True) + NORM_EPS) * head_norm.astype(jnp.float32)[None, :, None, :]
    h = h.transpose(0, 2, 1, 3).reshape(B, S, nv) * jax.nn.sigmoid(o.astype(jnp.float32))
    return h.astype(xn.dtype) @ w_out


def axial_rope(S):
    rows = S // GRID_W
    row = jnp.repeat(jnp.arange(rows, dtype=jnp.float32), GRID_W)
    col = jnp.tile(jnp.arange(GRID_W, dtype=jnp.float32), rows)
    n_freq = ATTN_HEAD_DIM // 4
    freqs = ROPE_THETA ** (-jnp.arange(n_freq, dtype=jnp.float32) / n_freq)
    ang = jnp.concatenate([row[:, None] * freqs, col[:, None] * freqs], axis=-1)
    return jnp.cos(ang), jnp.sin(ang)


def apply_rope(x, cos, sin):
    xp = x.reshape(*x.shape[:-1], -1, 2)
    x1, x2 = xp[..., 0], xp[..., 1]
    c = cos[None, :, None, :]
    s = sin[None, :, None, :]
    return jnp.stack([x1 * c - x2 * s, x1 * s + x2 * c], axis=-1).reshape(x.shape)


def attn_mixer(xn, w_in, q_norm, k_norm, w_out):
    B, S, _ = xn.shape
    nq = ATTN_Q_HEADS * ATTN_HEAD_DIM
    nkv = ATTN_KV_HEADS * ATTN_HEAD_DIM
    proj = xn @ w_in
    q, k, v = jnp.split(proj, [nq, nq + nkv], axis=-1)
    cos, sin = axial_rope(S)
    q = apply_rope(rms_norm(q.reshape(B, S, ATTN_Q_HEADS, ATTN_HEAD_DIM), q_norm).astype(jnp.float32), cos, sin)
    k = apply_rope(rms_norm(k.reshape(B, S, ATTN_KV_HEADS, ATTN_HEAD_DIM), k_norm).astype(jnp.float32), cos, sin)
    q = (q * (ATTN_HEAD_DIM ** -0.5)).astype(xn.dtype)
    k = k.astype(xn.dtype)
    v = v.reshape(B, S, ATTN_KV_HEADS, ATTN_HEAD_DIM)
    nb = S // ATTN_Q_BLOCK
    qb = jnp.moveaxis(q.reshape(B, nb, ATTN_Q_BLOCK, ATTN_KV_HEADS, ATTN_GROUPS, ATTN_HEAD_DIM), 1, 0)

    def block(q_blk):
        s = jnp.einsum('bqkgd,bskd->bkgqs', q_blk, k, preferred_element_type=jnp.float32)
        p = jax.nn.softmax(s, axis=-1).astype(v.dtype)
        return jnp.einsum('bkgqs,bskd->bqkgd', p, v)

    o = lax.map(block, qb)
    o = jnp.moveaxis(o, 0, 1).reshape(B, S, nq)
    return o @ w_out


def peer(xn, wq, k1, k2, u, v):
    B, S, D = xn.shape
    q = (xn @ wq).reshape(B, S, PEER_HEADS, 2, PEER_HALF)
    s1 = jnp.einsum('bshd,nd->bshn', q[..., 0, :], k1, preferred_element_type=jnp.float32)
    s2 = jnp.einsum('bshd,nd->bshn', q[..., 1, :], k2, preferred_element_type=jnp.float32)
    t1, i1 = lax.top_k(s1, PEER_TOPK)
    t2, i2 = lax.top_k(s2, PEER_TOPK)
    cand_s = (t1[..., :, None] + t2[..., None, :]).reshape(B, S, PEER_HEADS, PEER_TOPK * PEER_TOPK)
    cand_i = (i1[..., :, None] * PEER_N_KEYS + i2[..., None, :]).reshape(B, S, PEER_HEADS, PEER_TOPK * PEER_TOPK)
    top_s, pos = lax.top_k(cand_s, PEER_TOPK)
    idx = jnp.take_along_axis(cand_i, pos, axis=-1)
    gate = jax.nn.softmax(top_s, axis=-1)
    nblk = (B * S) // PEER_TOKEN_BLOCK
    E = PEER_HEADS * PEER_TOPK
    xb = xn.reshape(nblk, PEER_TOKEN_BLOCK, D)
    ib = idx.reshape(nblk, PEER_TOKEN_BLOCK, E)
    gb = gate.reshape(nblk, PEER_TOKEN_BLOCK, E)

    def block(args):
        x_blk, i_blk, g_blk = args
        act = jax.nn.gelu(jnp.einsum('td,ted->te', x_blk, u[i_blk], preferred_element_type=jnp.float32))
        return jnp.einsum('te,ted->td', (g_blk * act).astype(v.dtype), v[i_blk])

    return lax.map(block, (xb, ib, gb)).reshape(B, S, D)


def setup_inputs(seed: int = 0) -> dict:
    key = jax.random.key(seed)
    ks = jax.random.split(key, 17)
    n_a = (DEPTH + N_MIXERS - 1) // N_MIXERS
    n_b = DEPTH // N_MIXERS
    f32 = jnp.float32
    nrm = lambda k, shape, scale: jax.random.normal(k, shape, f32) * scale
    gain = lambda k, shape: 1.0 + 0.05 * jax.random.normal(k, shape, f32)
    nq = ATTN_Q_HEADS * ATTN_HEAD_DIM
    f_bias = jnp.linspace(MLSTM_F_BIAS_LO, MLSTM_F_BIAS_HI, MLSTM_HEADS, dtype=f32)
    base = jnp.array([0.0, 1.0, 0.0, 1.0], f32)[:, None] * f_bias[None, :]
    return {
        'x': nrm(ks[0], (BATCH, SEQ, D_MODEL), 1.0),
        'ln_mix': gain(ks[1], (DEPTH, D_MODEL)),
        'ln_ffn': gain(ks[2], (DEPTH, D_MODEL)),
        'mlstm_w_in': nrm(ks[3], (n_a, D_MODEL, MLSTM_IN), D_MODEL ** -0.5),
        'mlstm_gate_bias': base[None] + nrm(ks[4], (n_a, 4, MLSTM_HEADS), 0.1),
        'mlstm_head_norm': gain(ks[5], (n_a, MLSTM_HEADS, MLSTM_V_DIM)),
        'mlstm_w_out': nrm(ks[6], (n_a, MLSTM_HEADS * MLSTM_V_DIM, D_MODEL), (MLSTM_HEADS * MLSTM_V_DIM) ** -0.5),
        'attn_w_in': nrm(ks[7], (n_b, D_MODEL, ATTN_IN), D_MODEL ** -0.5),
        'attn_q_norm': gain(ks[8], (n_b, ATTN_HEAD_DIM)),
        'attn_k_norm': gain(ks[9], (n_b, ATTN_HEAD_DIM)),
        'attn_w_out': nrm(ks[10], (n_b, nq, D_MODEL), nq ** -0.5),
        'peer_wq': nrm(ks[11], (DEPTH, D_MODEL, PEER_HEADS * PEER_QUERY_DIM), D_MODEL ** -0.5),
        'peer_k1': nrm(ks[12], (DEPTH, PEER_N_KEYS, PEER_HALF), PEER_HALF ** -0.5),
        'peer_k2': nrm(ks[13], (DEPTH, PEER_N_KEYS, PEER_HALF), PEER_HALF ** -0.5),
        'peer_u': nrm(ks[14], (DEPTH, PEER_EXPERTS, D_MODEL), D_MODEL ** -0.5),
        'peer_v': nrm(ks[15], (DEPTH, PEER_EXPERTS, D_MODEL), PEER_V_SCALE),
    }


def reference(x, ln_mix, ln_ffn, mlstm_w_in, mlstm_gate_bias, mlstm_head_norm, mlstm_w_out,
              attn_w_in, attn_q_norm, attn_k_norm, attn_w_out,
              peer_wq, peer_k1, peer_k2, peer_u, peer_v):
    for layer in range(DEPTH):
        j = layer // N_MIXERS
        h = rms_norm(x, ln_mix[layer])
        if layer % N_MIXERS == 0:
            x = x + mlstm_mixer(h, mlstm_w_in[j], mlstm_gate_bias[j], mlstm_head_norm[j], mlstm_w_out[j])
        else:
            x = x + attn_mixer(h, attn_w_in[j], attn_q_norm[j], attn_k_norm[j], attn_w_out[j])
        h = rms_norm(x, ln_ffn[layer])
        x = x + peer(h, peer_wq[layer], peer_k1[layer], peer_k2[layer], peer_u[layer], peer_v[layer])
    return x
```

```python
import functools
import math

import jax
import jax.numpy as jnp
from jax import lax
from jax.experimental import pallas as pl
from jax.experimental.pallas import tpu as pltpu

NORM_EPS = 1e-6

MLSTM_HEADS = 4
MLSTM_QK_DIM = 256
MLSTM_V_DIM = 512

ATTN_HEAD_DIM = 128
ATTN_Q_HEADS = 16
ATTN_KV_HEADS = 4
ATTN_GROUPS = ATTN_Q_HEADS // ATTN_KV_HEADS
GRID_W = 64
ROPE_THETA = 10000.0

PEER_HEADS = 8
PEER_N_KEYS = 128
PEER_TOPK = 16
PEER_HALF = 128

LANES = 128
VMEM_LIMIT_BYTES = 56 * 1024 * 1024

BF16 = jnp.bfloat16
F32 = jnp.float32

_NT = (((1,), (1,)), ((), ()))


def _params(*sem):
    return pltpu.CompilerParams(dimension_semantics=sem, vmem_limit_bytes=VMEM_LIMIT_BYTES)


def _rms(x, g):
    return x * lax.rsqrt(jnp.mean(x * x, axis=-1, keepdims=True) + NORM_EPS) * g


def _norm_matmul_kernel(x_ref, g_ref, w_ref, o_ref, xn_ref):
    @pl.when(pl.program_id(1) == 0)
    def _():
        xn_ref[...] = _rms(x_ref[...], g_ref[...]).astype(BF16)

    o_ref[...] = jnp.dot(xn_ref[...], w_ref[...], preferred_element_type=F32).astype(o_ref.dtype)


def _norm_matmul_t_kernel(x_ref, g_ref, w_ref, o_ref, xt_ref, xn_ref):
    @pl.when(pl.program_id(1) == 0)
    def _():
        y = _rms(x_ref[...], g_ref[...])
        xn_ref[...] = y.astype(BF16)
        xt_ref[...] = y.T.astype(BF16)

    o_ref[...] = jnp.dot(xn_ref[...], w_ref[...], preferred_element_type=F32).astype(o_ref.dtype)


def norm_matmul(x, g, w, out_dtype, *, tm=512, tn=512, with_transposed_norm=False):
    m, k = x.shape
    n = w.shape[1]
    tn = min(tn, n)
    grid = (m // tm, n // tn)
    in_specs = [
        pl.BlockSpec((tm, k), lambda i, j: (i, 0)),
        pl.BlockSpec((1, k), lambda i, j: (0, 0)),
        pl.BlockSpec((k, tn), lambda i, j: (0, j)),
    ]
    o_spec = pl.BlockSpec((tm, tn), lambda i, j: (i, j))
    scratch = [pltpu.VMEM((tm, k), BF16)]
    if with_transposed_norm:
        return pl.pallas_call(
            _norm_matmul_t_kernel,
            grid=grid, in_specs=in_specs,
            out_specs=[o_spec, pl.BlockSpec((k, tm), lambda i, j: (0, i))],
            out_shape=[jax.ShapeDtypeStruct((m, n), out_dtype), jax.ShapeDtypeStruct((k, m), BF16)],
            scratch_shapes=scratch,
            compiler_params=_params("parallel", "arbitrary"),
        )(x, g.reshape(1, k), w)
    return pl.pallas_call(
        _norm_matmul_kernel,
        grid=grid, in_specs=in_specs, out_specs=o_spec,
        out_shape=jax.ShapeDtypeStruct((m, n), out_dtype),
        scratch_shapes=scratch,
        compiler_params=_params("parallel", "arbitrary"),
    )(x, g.reshape(1, k), w)


def _matmul_residual_kernel(x_ref, w_ref, r_ref, o_ref):
    o_ref[...] = r_ref[...] + jnp.dot(x_ref[...], w_ref[...], preferred_element_type=F32)


def matmul_residual(x, w, res, *, tm=512, tn=512):
    m, k = x.shape
    n = w.shape[1]
    return pl.pallas_call(
        _matmul_residual_kernel,
        grid=(m // tm, n // tn),
        in_specs=[
            pl.BlockSpec((tm, k), lambda i, j: (i, 0)),
            pl.BlockSpec((k, tn), lambda i, j: (0, j)),
            pl.BlockSpec((tm, tn), lambda i, j: (i, j)),
        ],
        out_specs=pl.BlockSpec((tm, tn), lambda i, j: (i, j)),
        out_shape=jax.ShapeDtypeStruct((m, n), F32),
        compiler_params=_params("parallel", "arbitrary"),
    )(x, w, res)


def _scan_lanes(x, op, fill, is_suffix):
    s = x.shape[-1]
    lane = lax.broadcasted_iota(jnp.int32, x.shape, 1)
    d = 1
    while d < s:
        from_left = jnp.where(lane >= d, pltpu.roll(x, d, axis=1), fill)
        from_right = jnp.where(lane < s - d, pltpu.roll(x, s - d, axis=1), fill)
        x = op(x, jnp.where(is_suffix, from_right, from_left))
        d *= 2
    return x


def _gate_scan_kernel(ig_ref, fg_ref, bi_ref, bf_ref, a_ref, m_ref, e_ref):
    ig = ig_ref[0] + bi_ref[...]
    fg = fg_ref[0] + bf_ref[...]
    is_suffix = lax.broadcasted_iota(jnp.int32, ig.shape, 0) >= MLSTM_HEADS
    logsig = jnp.minimum(fg, 0.0) - jnp.log1p(jnp.exp(-jnp.abs(fg)))
    fcum = _scan_lanes(logsig, jnp.add, 0.0, is_suffix)
    a = ig - fcum
    mx = jnp.maximum(_scan_lanes(a, jnp.maximum, -jnp.inf, is_suffix), 0.0)
    a_ref[0] = a
    m_ref[0] = mx
    e_ref[0] = jnp.exp(-(fcum + mx))


def gate_scan(ig, fg, bi, bf):
    b, r, s = ig.shape
    g_spec = pl.BlockSpec((1, r, s), lambda i: (i, 0, 0))
    b_spec = pl.BlockSpec((r, 1), lambda i: (0, 0))
    o_shape = jax.ShapeDtypeStruct((b, r, s), F32)
    return pl.pallas_call(
        _gate_scan_kernel,
        grid=(b,),
        in_specs=[g_spec, g_spec, b_spec, b_spec],
        out_specs=[g_spec] * 3,
        out_shape=[o_shape] * 3,
        compiler_params=_params("parallel"),
    )(ig, fg, bi, bf)


def _mlstm_kernel(q_ref, k_ref, v_ref, o_ref, row_ref, col_ref, hn_ref, out_ref,
                  numf_ref, numb_ref, denf_ref, denb_ref, *, tq, nk):
    i = pl.program_id(2)
    q = q_ref[...]
    col = col_ref[0, 0]
    m_f, m_b, e_f, e_b = col[:, 0:1], col[:, 1:2], col[:, 2:3], col[:, 3:4]
    scale = MLSTM_QK_DIM ** -0.5

    numf_ref[...] = jnp.zeros_like(numf_ref)
    numb_ref[...] = jnp.zeros_like(numb_ref)
    denf_ref[...] = jnp.zeros_like(denf_ref)
    denb_ref[...] = jnp.zeros_like(denb_ref)

    def scores(j):
        start = pl.multiple_of(j * tq, tq)
        kj = k_ref[pl.ds(start, tq), :]
        vj = v_ref[pl.ds(start, tq), :]
        return lax.dot_general(q, kj, _NT, preferred_element_type=F32) * scale, vj

    def accumulate(s, vj, num_ref, den_ref):
        den_ref[...] += jnp.sum(s, axis=-1, keepdims=True)
        num_ref[...] += jnp.dot(s.astype(BF16), vj, preferred_element_type=F32)

    def fwd_block(j, carry):
        s, vj = scores(j)
        a = row_ref[0, 0, j][0:1, :]
        accumulate(s * jnp.exp(a - m_f), vj, numf_ref, denf_ref)
        return carry

    def bwd_block(j, carry):
        s, vj = scores(j)
        a = row_ref[0, 0, j][1:2, :]
        accumulate(s * jnp.exp(a - m_b), vj, numb_ref, denb_ref)
        return carry

    lax.fori_loop(0, i, fwd_block, 0)
    lax.fori_loop(i + 1, nk, bwd_block, 0)

    s, vj = scores(i)
    rows = row_ref[0, 0, i]
    t_idx = lax.broadcasted_iota(jnp.int32, (tq, tq), 0)
    s_idx = lax.broadcasted_iota(jnp.int32, (tq, tq), 1)
    d_f = jnp.where(s_idx <= t_idx, jnp.exp(rows[0:1, :] - m_f), 0.0)
    d_b = jnp.where(s_idx >= t_idx, jnp.exp(rows[1:2, :] - m_b), 0.0)
    accumulate(s * d_f, vj, numf_ref, denf_ref)
    accumulate(s * d_b, vj, numb_ref, denb_ref)

    h = (numf_ref[...] / jnp.maximum(jnp.abs(denf_ref[...]), e_f)
         + numb_ref[...] / jnp.maximum(jnp.abs(denb_ref[...]), e_b))
    h = _rms(h, hn_ref[0])
    out_ref[...] = (h * jax.nn.sigmoid(o_ref[...].astype(F32))).astype(out_ref.dtype)


def mlstm_core(proj, rowp, colp, head_norm, batch, seq, *, tq=256):
    h, dk, dv = MLSTM_HEADS, MLSTM_QK_DIM, MLSTM_V_DIM
    nq = seq // tq
    k_col0 = h * dk // dk
    v_col0 = 2 * h * dk // dv
    o_col0 = (2 * h * dk + h * dv) // dv
    kernel = functools.partial(_mlstm_kernel, tq=tq, nk=nq)
    return pl.pallas_call(
        kernel,
        grid=(batch, h, nq),
        in_specs=[
            pl.BlockSpec((tq, dk), lambda b, hh, i: (b * nq + i, hh)),
            pl.BlockSpec((seq, dk), lambda b, hh, i: (b, k_col0 + hh)),
            pl.BlockSpec((seq, dv), lambda b, hh, i: (b, v_col0 + hh)),
            pl.BlockSpec((tq, dv), lambda b, hh, i: (b * nq + i, o_col0 + hh)),
            pl.BlockSpec((1, 1, nq, 8, tq), lambda b, hh, i: (b, hh, 0, 0, 0)),
            pl.BlockSpec((1, 1, tq, 8), lambda b, hh, i: (b, hh, i, 0)),
            pl.BlockSpec((1, 1, dv), lambda b, hh, i: (hh, 0, 0)),
        ],
        out_specs=pl.BlockSpec((tq, dv), lambda b, hh, i: (b * nq + i, hh)),
        out_shape=jax.ShapeDtypeStruct((batch * seq, h * dv), BF16),
        scratch_shapes=[pltpu.VMEM((tq, dv), F32), pltpu.VMEM((tq, dv), F32),
                        pltpu.VMEM((tq, 1), F32), pltpu.VMEM((tq, 1), F32)],
        compiler_params=_params("parallel", "parallel", "arbitrary"),
    )(proj, proj, proj, proj, rowp, colp, head_norm.reshape(h, 1, dv))


def mlstm_mixer(x2d, ln, w_in, gate_bias, head_norm, w_out, batch, seq):
    h = MLSTM_HEADS
    n_main = 2 * h * MLSTM_QK_DIM + 2 * h * MLSTM_V_DIM
    w_main = w_in[:, :n_main].astype(BF16)
    w_gate = jnp.pad(w_in[:, n_main:], ((0, 0), (0, LANES - 4 * h))).astype(BF16)
    proj = norm_matmul(x2d, ln, w_main, BF16, tn=1024)
    gates = norm_matmul(x2d, ln, w_gate, F32)[:, :4 * h]
    gates_t = gates.reshape(batch, seq, 4, h).transpose(0, 2, 3, 1)
    bias = gate_bias.astype(F32)
    a, mx, en = gate_scan(gates_t[:, 0::2].reshape(batch, 2 * h, seq), gates_t[:, 1::2].reshape(batch, 2 * h, seq),
                          bias[0::2].reshape(2 * h, 1), bias[1::2].reshape(2 * h, 1))
    tq = 256
    nk = seq // tq
    rowp = a.reshape(batch, 2, h, nk, tq).transpose(0, 2, 3, 1, 4)
    rowp = jnp.pad(rowp, ((0, 0), (0, 0), (0, 0), (0, 6), (0, 0)))
    colp = jnp.concatenate([mx.reshape(batch, 2, h, seq), en.reshape(batch, 2, h, seq)], axis=1)
    colp = jnp.pad(colp.transpose(0, 2, 3, 1), ((0, 0), (0, 0), (0, 0), (0, 4)))
    hg = mlstm_core(proj, rowp, colp, head_norm, batch, seq, tq=tq)
    return matmul_residual(hg, w_out.astype(BF16), x2d)


def _qk_proj_kernel(x_ref, g_ref, w_ref, gain_ref, mult_ref, cos_ref, sin_ref, o_ref, xn_ref):
    @pl.when(pl.program_id(1) == 0)
    def _():
        xn_ref[...] = _rms(x_ref[...], g_ref[...]).astype(BF16)

    acc = jnp.dot(xn_ref[...], w_ref[...], preferred_element_type=F32)
    hd = ATTN_HEAD_DIM
    cos = cos_ref[...]
    sin = sin_ref[...]
    for c in range(acc.shape[1] // hd):
        sl = slice(c * hd, (c + 1) * hd)
        y = _rms(acc[:, sl], gain_ref[:, sl])
        y = y * cos + pltpu.roll(y, hd // 2, axis=1) * sin
        o_ref[:, sl] = (y * mult_ref[:, sl]).astype(o_ref.dtype)


def qk_proj(x, g, w, gain, mult, cos_t, sin_t, seq, *, tm=512, tn=512):
    m, k = x.shape
    n = w.shape[1]
    nseq = seq // tm
    return pl.pallas_call(
        _qk_proj_kernel,
        grid=(m // tm, n // tn),
        in_specs=[
            pl.BlockSpec((tm, k), lambda i, j: (i, 0)),
            pl.BlockSpec((1, k), lambda i, j: (0, 0)),
            pl.BlockSpec((k, tn), lambda i, j: (0, j)),
            pl.BlockSpec((1, tn), lambda i, j: (0, j)),
            pl.BlockSpec((1, tn), lambda i, j: (0, j)),
            pl.BlockSpec((tm, ATTN_HEAD_DIM), lambda i, j: (i % nseq, 0)),
            pl.BlockSpec((tm, ATTN_HEAD_DIM), lambda i, j: (i % nseq, 0)),
        ],
        out_specs=pl.BlockSpec((tm, tn), lambda i, j: (i, j)),
        out_shape=jax.ShapeDtypeStruct((m, n), BF16),
        scratch_shapes=[pltpu.VMEM((tm, k), BF16)],
        compiler_params=_params("parallel", "arbitrary"),
    )(x, g.reshape(1, k), w, gain, mult, cos_t, sin_t)


def _attn_kernel(q_ref, k_ref, v_ref, o_ref):
    hd = ATTN_HEAD_DIM
    k = k_ref[...]
    v = v_ref[...]
    for g in range(ATTN_GROUPS):
        sl = slice(g * hd, (g + 1) * hd)
        s = lax.dot_general(q_ref[:, sl], k, _NT, preferred_element_type=F32)
        p = jnp.exp(s - jnp.max(s, axis=-1, keepdims=True))
        l = jnp.sum(p, axis=-1, keepdims=True)
        o = jnp.dot(p.astype(BF16), v, preferred_element_type=F32)
        o_ref[:, sl] = (o / l).astype(o_ref.dtype)


def attn_core(qk, v, batch, seq, *, tq=512):
    hd, g = ATTN_HEAD_DIM, ATTN_GROUPS
    nq = seq // tq
    return pl.pallas_call(
        _attn_kernel,
        grid=(batch, ATTN_KV_HEADS, nq),
        in_specs=[
            pl.BlockSpec((tq, g * hd), lambda b, kv, i: (b * nq + i, kv)),
            pl.BlockSpec((seq, hd), lambda b, kv, i: (b, ATTN_Q_HEADS + kv)),
            pl.BlockSpec((seq, hd), lambda b, kv, i: (b, kv)),
        ],
        out_specs=pl.BlockSpec((tq, g * hd), lambda b, kv, i: (b * nq + i, kv)),
        out_shape=jax.ShapeDtypeStruct((batch * seq, ATTN_Q_HEADS * hd), BF16),
        compiler_params=_params("parallel", "parallel", "arbitrary"),
    )(qk, qk, v)


def _rope_tables(seq):
    rows = seq // GRID_W
    row = jnp.repeat(jnp.arange(rows, dtype=F32), GRID_W)
    col = jnp.tile(jnp.arange(GRID_W, dtype=F32), rows)
    n_freq = ATTN_HEAD_DIM // 4
    freqs = ROPE_THETA ** (-jnp.arange(n_freq, dtype=F32) / n_freq)
    ang = jnp.concatenate([row[:, None] * freqs, col[:, None] * freqs], axis=-1)
    cos, sin = jnp.cos(ang), jnp.sin(ang)
    return jnp.concatenate([cos, cos], axis=-1), jnp.concatenate([-sin, sin], axis=-1)


def attn_mixer(x2d, ln, w_in, q_norm, k_norm, w_out, batch, seq):
    hd = ATTN_HEAD_DIM
    n_qk_heads = ATTN_Q_HEADS + ATTN_KV_HEADS
    n_qk = n_qk_heads * hd
    perm = jnp.concatenate([jnp.arange(0, hd, 2), jnp.arange(1, hd, 2)])
    d_model = w_in.shape[0]
    w_qk = w_in[:, :n_qk].reshape(d_model, n_qk_heads, hd)[:, :, perm].reshape(d_model, n_qk).astype(BF16)
    w_v = w_in[:, n_qk:].astype(BF16)
    gain = jnp.concatenate([jnp.tile(q_norm[perm], ATTN_Q_HEADS), jnp.tile(k_norm[perm], ATTN_KV_HEADS)])
    mult = jnp.concatenate([jnp.full((ATTN_Q_HEADS * hd,), hd ** -0.5, F32),
                            jnp.ones((ATTN_KV_HEADS * hd,), F32)])
    cos_t, sin_t = _rope_tables(seq)
    qk = qk_proj(x2d, ln, w_qk, gain.reshape(1, n_qk).astype(F32), mult.reshape(1, n_qk), cos_t, sin_t, seq)
    v = norm_matmul(x2d, ln, w_v, BF16)
    o = attn_core(qk, v, batch, seq)
    return matmul_residual(o, w_out.astype(BF16), x2d)


_PEER_CAND = [(i, j) for i in range(PEER_TOPK) for j in range(min(PEER_TOPK, (PEER_TOPK + 1) // (i + 1)))]
_PEER_CAND_ROWS = -(-len(_PEER_CAND) // 8) * 8


def _top_values(x, n):
    vals = []
    for _ in range(n):
        m = jnp.max(x, axis=0, keepdims=True)
        vals.append(m)
        x = jnp.where(x == m, -jnp.inf, x)
    return vals


def _peer_route_kernel(q_ref, k1_ref, k2_ref, thr_ref, c_ref, e2_ref, s2_ref, cand_ref):
    nk, half, topk = PEER_N_KEYS, PEER_HALF, PEER_TOPK
    k1 = k1_ref[...]
    k2 = k2_ref[...]
    tm = q_ref.shape[0]
    cand_ref[...] = jnp.full(cand_ref.shape, -jnp.inf, F32)
    for h in range(PEER_HEADS):
        qa = q_ref[:, (2 * h) * half:(2 * h + 1) * half]
        qb = q_ref[:, (2 * h + 1) * half:(2 * h + 2) * half]
        s1 = lax.dot_general(k1, qa, _NT, preferred_element_type=F32)
        s2 = lax.dot_general(k2, qb, _NT, preferred_element_type=F32)
        t1 = _top_values(s1, topk + 1)
        t2 = _top_values(s2, topk + 1)
        for r, (i, j) in enumerate(_PEER_CAND):
            cand_ref[r:r + 1, :] = t1[i] + t2[j]
        c = _top_values(cand_ref[...], topk + 1)
        c_next = jnp.maximum(c[topk], jnp.maximum(t1[topk] + t2[0], t1[0] + t2[topk]))
        thr = 0.5 * (c[topk - 1] + c_next)
        z = jnp.zeros((1, tm), F32)
        for kk in range(topk):
            z = z + jnp.exp(c[kk] - c[0])
        rows = slice(h * nk, (h + 1) * nk)
        thr_ref[rows, :] = thr - s1
        c_ref[rows, :] = jnp.exp(s1 - t1[0]) / z
        e2_ref[rows, :] = jnp.exp(s2 - t2[0])
        s2_ref[rows, :] = s2


def peer_route(q, k1, k2, *, tm=512):
    t, d = q.shape
    rows = PEER_HEADS * PEER_N_KEYS
    o_spec = pl.BlockSpec((rows, tm), lambda i: (0, i))
    o_shape = jax.ShapeDtypeStruct((rows, t), F32)
    return pl.pallas_call(
        _peer_route_kernel,
        grid=(t // tm,),
        in_specs=[
            pl.BlockSpec((tm, d), lambda i: (i, 0)),
            pl.BlockSpec((PEER_N_KEYS, PEER_HALF), lambda i: (0, 0)),
            pl.BlockSpec((PEER_N_KEYS, PEER_HALF), lambda i: (0, 0)),
        ],
        out_specs=[o_spec] * 4,
        out_shape=[o_shape] * 4,
        scratch_shapes=[pltpu.VMEM((_PEER_CAND_ROWS, tm), F32)],
        compiler_params=_params("parallel"),
    )(q, k1, k2)


def _gelu_tanh(x):
    return 0.5 * x * (1.0 + jnp.tanh(math.sqrt(2.0 / math.pi) * (x + 0.044715 * (x * x * x))))


def _peer_expert_kernel(xt_ref, u_ref, vt_ref, thr_ref, c_ref, e2_ref, s2_ref, res_ref, o_ref,
                        acc_ref, w_ref, *, eb):
    e = pl.program_id(1)
    nk = PEER_N_KEYS
    n_sub = eb // nk

    @pl.when(e == 0)
    def _():
        acc_ref[...] = jnp.zeros_like(acc_ref)

    act = _gelu_tanh(jnp.dot(u_ref[...], xt_ref[...], preferred_element_type=F32))
    for a in range(n_sub):
        gate = jnp.zeros((nk, act.shape[1]), F32)
        for h in range(PEER_HEADS):
            base = pl.multiple_of(h * nk + e * n_sub, n_sub)
            thr = thr_ref[pl.ds(base, n_sub), :][a:a + 1, :]
            c = c_ref[pl.ds(base, n_sub), :][a:a + 1, :]
            rows = slice(h * nk, (h + 1) * nk)
            gate = gate + jnp.where(s2_ref[rows, :] >= thr, e2_ref[rows, :] * c, 0.0)
        w_ref[a * nk:(a + 1) * nk, :] = (gate * act[a * nk:(a + 1) * nk, :]).astype(BF16)
    acc_ref[...] += jnp.dot(vt_ref[...], w_ref[...], preferred_element_type=F32)

    @pl.when(e == pl.num_programs(1) - 1)
    def _():
        o_ref[...] = res_ref[...] + acc_ref[...].T


def peer_experts(xt, u, vt, thr, c, e2, s2, res, *, tt=256, eb=1024):
    d, t = xt.shape
    n_exp = u.shape[0]
    rows = PEER_HEADS * PEER_N_KEYS
    side = pl.BlockSpec((rows, tt), lambda i, e: (0, i))
    kernel = functools.partial(_peer_expert_kernel, eb=eb)
    return pl.pallas_call(
        kernel,
        grid=(t // tt, n_exp // eb),
        in_specs=[
            pl.BlockSpec((d, tt), lambda i, e: (0, i)),
            pl.BlockSpec((eb, d), lambda i, e: (e, 0)),
            pl.BlockSpec((d, eb), lambda i, e: (0, e)),
            side, side, side, side,
            pl.BlockSpec((tt, d), lambda i, e: (i, 0)),
        ],
        out_specs=pl.BlockSpec((tt, d), lambda i, e: (i, 0)),
        out_shape=jax.ShapeDtypeStruct((t, d), F32),
        scratch_shapes=[pltpu.VMEM((d, tt), F32), pltpu.VMEM((eb, tt), BF16)],
        compiler_params=_params("parallel", "arbitrary"),
    )(xt, u, vt, thr, c, e2, s2, res)


def peer_mixer(x2d, ln, wq, k1, k2, u, v):
    q, xt = norm_matmul(x2d, ln, wq.astype(BF16), BF16, with_transposed_norm=True)
    thr, c, e2, s2 = peer_route(q, k1.astype(BF16), k2.astype(BF16))
    return peer_experts(xt, u.astype(BF16), v.T.astype(BF16), thr, c, e2, s2, x2d)


def kernel(x, ln_mix, ln_ffn, mlstm_w_in, mlstm_gate_bias, mlstm_head_norm, mlstm_w_out,
           attn_w_in, attn_q_norm, attn_k_norm, attn_w_out,
           peer_wq, peer_k1, peer_k2, peer_u, peer_v):
    batch, seq, d_model = x.shape
    depth = ln_mix.shape[0]
    x2d = x.reshape(batch * seq, d_model)
    for layer in range(depth):
        j = layer // 2
        if layer % 2 == 0:
            x2d = mlstm_mixer(x2d, ln_mix[layer], mlstm_w_in[j], mlstm_gate_bias[j], mlstm_head_norm[j],
                              mlstm_w_out[j], batch, seq)
        else:
            x2d = attn_mixer(x2d, ln_mix[layer], attn_w_in[j], attn_q_norm[j], attn_k_norm[j],
                             attn_w_out[j], batch, seq)
        x2d = peer_mixer(x2d, ln_ffn[layer], peer_wq[layer], peer_k1[layer], peer_k2[layer],
                         peer_u[layer], peer_v[layer])
    return x2d.reshape(batch, seq, d_model)
```

```python
import functools
import math

import jax
import jax.numpy as jnp
from jax import lax
from jax.experimental import pallas as pl
from jax.experimental.pallas import tpu as pltpu

NORM_EPS = 1e-6

MLSTM_HEADS = 4
MLSTM_QK_DIM = 256
MLSTM_V_DIM = 512

ATTN_HEAD_DIM = 128
ATTN_Q_HEADS = 16
ATTN_KV_HEADS = 4
ATTN_GROUPS = ATTN_Q_HEADS // ATTN_KV_HEADS
GRID_W = 64
ROPE_THETA = 10000.0

PEER_HEADS = 8
PEER_N_KEYS = 128
PEER_TOPK = 16
PEER_HALF = 128
PEER_EXPERT_BLOCK = 1024
PEER_EXPERT_PIECE = 256

LANES = 128
VMEM_LIMIT_BYTES = 56 * 1024 * 1024

BF16 = jnp.bfloat16
F32 = jnp.float32

_NT = (((1,), (1,)), ((), ()))


def _params(*sem):
    return pltpu.CompilerParams(dimension_semantics=sem, vmem_limit_bytes=VMEM_LIMIT_BYTES)


def _rms(x, g):
    return x * lax.rsqrt(jnp.mean(x * x, axis=-1, keepdims=True) + NORM_EPS) * g


def _norm_matmul_kernel(x_ref, g_ref, w_ref, o_ref, xn_ref):
    @pl.when(pl.program_id(1) == 0)
    def _():
        xn_ref[...] = _rms(x_ref[...], g_ref[...]).astype(BF16)

    o_ref[...] = jnp.dot(xn_ref[...], w_ref[...], preferred_element_type=F32).astype(o_ref.dtype)


def _norm_matmul_t_kernel(x_ref, g_ref, w_ref, o_ref, xt_ref, xn_ref):
    @pl.when(pl.program_id(1) == 0)
    def _():
        y = _rms(x_ref[...], g_ref[...])
        xn_ref[...] = y.astype(BF16)
        xt_ref[...] = y.T.astype(BF16)

    o_ref[...] = jnp.dot(xn_ref[...], w_ref[...], preferred_element_type=F32).astype(o_ref.dtype)


def norm_matmul(x, g, w, out_dtype, *, tm=512, tn=512, with_transposed_norm=False):
    m, k = x.shape
    n = w.shape[1]
    tn = min(tn, n)
    grid = (m // tm, n // tn)
    in_specs = [
        pl.BlockSpec((tm, k), lambda i, j: (i, 0)),
        pl.BlockSpec((1, k), lambda i, j: (0, 0)),
        pl.BlockSpec((k, tn), lambda i, j: (0, j)),
    ]
    o_spec = pl.BlockSpec((tm, tn), lambda i, j: (i, j))
    scratch = [pltpu.VMEM((tm, k), BF16)]
    if with_transposed_norm:
        return pl.pallas_call(
            _norm_matmul_t_kernel,
            grid=grid, in_specs=in_specs,
            out_specs=[o_spec, pl.BlockSpec((k, tm), lambda i, j: (0, i))],
            out_shape=[jax.ShapeDtypeStruct((m, n), out_dtype), jax.ShapeDtypeStruct((k, m), BF16)],
            scratch_shapes=scratch,
            compiler_params=_params("parallel", "arbitrary"),
        )(x, g.reshape(1, k), w)
    return pl.pallas_call(
        _norm_matmul_kernel,
        grid=grid, in_specs=in_specs, out_specs=o_spec,
        out_shape=jax.ShapeDtypeStruct((m, n), out_dtype),
        scratch_shapes=scratch,
        compiler_params=_params("parallel", "arbitrary"),
    )(x, g.reshape(1, k), w)


def _matmul_residual_kernel(x_ref, w_ref, r_ref, o_ref):
    o_ref[...] = r_ref[...] + jnp.dot(x_ref[...], w_ref[...], preferred_element_type=F32)


def matmul_residual(x, w, res, *, tm=512, tn=512):
    m, k = x.shape
    n = w.shape[1]
    return pl.pallas_call(
        _matmul_residual_kernel,
        grid=(m // tm, n // tn),
        in_specs=[
            pl.BlockSpec((tm, k), lambda i, j: (i, 0)),
            pl.BlockSpec((k, tn), lambda i, j: (0, j)),
            pl.BlockSpec((tm, tn), lambda i, j: (i, j)),
        ],
        out_specs=pl.BlockSpec((tm, tn), lambda i, j: (i, j)),
        out_shape=jax.ShapeDtypeStruct((m, n), F32),
        compiler_params=_params("parallel", "arbitrary"),
    )(x, w, res)


def _scan_lanes(x, op, fill, is_suffix):
    s = x.shape[-1]
    lane = lax.broadcasted_iota(jnp.int32, x.shape, 1)
    d = 1
    while d < s:
        from_left = jnp.where(lane >= d, pltpu.roll(x, d, axis=1), fill)
        from_right = jnp.where(lane < s - d, pltpu.roll(x, s - d, axis=1), fill)
        x = op(x, jnp.where(is_suffix, from_right, from_left))
        d *= 2
    return x


def _gate_scan_kernel(ig_ref, fg_ref, bi_ref, bf_ref, a_ref, m_ref, e_ref):
    ig = ig_ref[0] + bi_ref[...]
    fg = fg_ref[0] + bf_ref[...]
    is_suffix = lax.broadcasted_iota(jnp.int32, ig.shape, 0) >= MLSTM_HEADS
    logsig = jnp.minimum(fg, 0.0) - jnp.log1p(jnp.exp(-jnp.abs(fg)))
    fcum = _scan_lanes(logsig, jnp.add, 0.0, is_suffix)
    a = ig - fcum
    mx = jnp.maximum(_scan_lanes(a, jnp.maximum, -jnp.inf, is_suffix), 0.0)
    a_ref[0] = a
    m_ref[0] = mx
    e_ref[0] = jnp.exp(-(fcum + mx))


def gate_scan(ig, fg, bi, bf):
    b, r, s = ig.shape
    g_spec = pl.BlockSpec((1, r, s), lambda i: (i, 0, 0))
    b_spec = pl.BlockSpec((r, 1), lambda i: (0, 0))
    o_shape = jax.ShapeDtypeStruct((b, r, s), F32)
    return pl.pallas_call(
        _gate_scan_kernel,
        grid=(b,),
        in_specs=[g_spec, g_spec, b_spec, b_spec],
        out_specs=[g_spec] * 3,
        out_shape=[o_shape] * 3,
        compiler_params=_params("parallel"),
    )(ig, fg, bi, bf)


def _mlstm_kernel(q_ref, k_ref, v_ref, o_ref, row_ref, col_ref, hn_ref, out_ref,
                  numf_ref, numb_ref, denf_ref, denb_ref, *, tq, nk):
    i = pl.program_id(2)
    q = q_ref[...]
    col = col_ref[0, 0]
    m_f, m_b, e_f, e_b = col[:, 0:1], col[:, 1:2], col[:, 2:3], col[:, 3:4]
    scale = MLSTM_QK_DIM ** -0.5

    numf_ref[...] = jnp.zeros_like(numf_ref)
    numb_ref[...] = jnp.zeros_like(numb_ref)
    denf_ref[...] = jnp.zeros_like(denf_ref)
    denb_ref[...] = jnp.zeros_like(denb_ref)

    def scores(j):
        start = pl.multiple_of(j * tq, tq)
        kj = k_ref[pl.ds(start, tq), :]
        vj = v_ref[pl.ds(start, tq), :]
        return lax.dot_general(q, kj, _NT, preferred_element_type=F32) * scale, vj

    def accumulate(s, vj, num_ref, den_ref):
        den_ref[...] += jnp.sum(s, axis=-1, keepdims=True)
        num_ref[...] += jnp.dot(s.astype(BF16), vj, preferred_element_type=F32)

    def fwd_block(j, carry):
        s, vj = scores(j)
        a = row_ref[0, 0, j][0:1, :]
        accumulate(s * jnp.exp(a - m_f), vj, numf_ref, denf_ref)
        return carry

    def bwd_block(j, carry):
        s, vj = scores(j)
        a = row_ref[0, 0, j][1:2, :]
        accumulate(s * jnp.exp(a - m_b), vj, numb_ref, denb_ref)
        return carry

    lax.fori_loop(0, i, fwd_block, 0)
    lax.fori_loop(i + 1, nk, bwd_block, 0)

    s, vj = scores(i)
    rows = row_ref[0, 0, i]
    t_idx = lax.broadcasted_iota(jnp.int32, (tq, tq), 0)
    s_idx = lax.broadcasted_iota(jnp.int32, (tq, tq), 1)
    d_f = jnp.where(s_idx <= t_idx, jnp.exp(rows[0:1, :] - m_f), 0.0)
    d_b = jnp.where(s_idx >= t_idx, jnp.exp(rows[1:2, :] - m_b), 0.0)
    accumulate(s * d_f, vj, numf_ref, denf_ref)
    accumulate(s * d_b, vj, numb_ref, denb_ref)

    h = (numf_ref[...] / jnp.maximum(jnp.abs(denf_ref[...]), e_f)
         + numb_ref[...] / jnp.maximum(jnp.abs(denb_ref[...]), e_b))
    h = _rms(h, hn_ref[0])
    out_ref[...] = (h * jax.nn.sigmoid(o_ref[...].astype(F32))).astype(out_ref.dtype)


def mlstm_core(proj, rowp, colp, head_norm, batch, seq, *, tq=256):
    h, dk, dv = MLSTM_HEADS, MLSTM_QK_DIM, MLSTM_V_DIM
    nq = seq // tq
    k_col0 = h * dk // dk
    v_col0 = 2 * h * dk // dv
    o_col0 = (2 * h * dk + h * dv) // dv
    kernel = functools.partial(_mlstm_kernel, tq=tq, nk=nq)
    return pl.pallas_call(
        kernel,
        grid=(batch, h, nq),
        in_specs=[
            pl.BlockSpec((tq, dk), lambda b, hh, i: (b * nq + i, hh)),
            pl.BlockSpec((seq, dk), lambda b, hh, i: (b, k_col0 + hh)),
            pl.BlockSpec((seq, dv), lambda b, hh, i: (b, v_col0 + hh)),
            pl.BlockSpec((tq, dv), lambda b, hh, i: (b * nq + i, o_col0 + hh)),
            pl.BlockSpec((1, 1, nq, 8, tq), lambda b, hh, i: (b, hh, 0, 0, 0)),
            pl.BlockSpec((1, 1, tq, 8), lambda b, hh, i: (b, hh, i, 0)),
            pl.BlockSpec((1, 1, dv), lambda b, hh, i: (hh, 0, 0)),
        ],
        out_specs=pl.BlockSpec((tq, dv), lambda b, hh, i: (b * nq + i, hh)),
        out_shape=jax.ShapeDtypeStruct((batch * seq, h * dv), BF16),
        scratch_shapes=[pltpu.VMEM((tq, dv), F32), pltpu.VMEM((tq, dv), F32),
                        pltpu.VMEM((tq, 1), F32), pltpu.VMEM((tq, 1), F32)],
        compiler_params=_params("parallel", "parallel", "arbitrary"),
    )(proj, proj, proj, proj, rowp, colp, head_norm.reshape(h, 1, dv))


def mlstm_mixer(x2d, ln, w_in, gate_bias, head_norm, w_out, batch, seq):
    h = MLSTM_HEADS
    n_main = 2 * h * MLSTM_QK_DIM + 2 * h * MLSTM_V_DIM
    w_main = w_in[:, :n_main].astype(BF16)
    w_gate = jnp.pad(w_in[:, n_main:], ((0, 0), (0, LANES - 4 * h))).astype(BF16)
    proj = norm_matmul(x2d, ln, w_main, BF16, tn=1024)
    gates = norm_matmul(x2d, ln, w_gate, F32)[:, :4 * h]
    gates_t = gates.reshape(batch, seq, 4, h).transpose(0, 2, 3, 1)
    bias = gate_bias.astype(F32)
    a, mx, en = gate_scan(gates_t[:, 0::2].reshape(batch, 2 * h, seq), gates_t[:, 1::2].reshape(batch, 2 * h, seq),
                          bias[0::2].reshape(2 * h, 1), bias[1::2].reshape(2 * h, 1))
    tq = 256
    nk = seq // tq
    rowp = a.reshape(batch, 2, h, nk, tq).transpose(0, 2, 3, 1, 4)
    rowp = jnp.pad(rowp, ((0, 0), (0, 0), (0, 0), (0, 6), (0, 0)))
    colp = jnp.concatenate([mx.reshape(batch, 2, h, seq), en.reshape(batch, 2, h, seq)], axis=1)
    colp = jnp.pad(colp.transpose(0, 2, 3, 1), ((0, 0), (0, 0), (0, 0), (0, 4)))
    hg = mlstm_core(proj, rowp, colp, head_norm, batch, seq, tq=tq)
    return matmul_residual(hg, w_out.astype(BF16), x2d)


def _qk_proj_kernel(x_ref, g_ref, w_ref, gain_ref, mult_ref, cos_ref, sin_ref, o_ref, xn_ref):
    @pl.when(pl.program_id(1) == 0)
    def _():
        xn_ref[...] = _rms(x_ref[...], g_ref[...]).astype(BF16)

    acc = jnp.dot(xn_ref[...], w_ref[...], preferred_element_type=F32)
    hd = ATTN_HEAD_DIM
    cos = cos_ref[...]
    sin = sin_ref[...]
    for c in range(acc.shape[1] // hd):
        sl = slice(c * hd, (c + 1) * hd)
        y = _rms(acc[:, sl], gain_ref[:, sl])
        y = y * cos + pltpu.roll(y, hd // 2, axis=1) * sin
        o_ref[:, sl] = (y * mult_ref[:, sl]).astype(o_ref.dtype)


def qk_proj(x, g, w, gain, mult, cos_t, sin_t, seq, *, tm=512, tn=512):
    m, k = x.shape
    n = w.shape[1]
    nseq = seq // tm
    return pl.pallas_call(
        _qk_proj_kernel,
        grid=(m // tm, n // tn),
        in_specs=[
            pl.BlockSpec((tm, k), lambda i, j: (i, 0)),
            pl.BlockSpec((1, k), lambda i, j: (0, 0)),
            pl.BlockSpec((k, tn), lambda i, j: (0, j)),
            pl.BlockSpec((1, tn), lambda i, j: (0, j)),
            pl.BlockSpec((1, tn), lambda i, j: (0, j)),
            pl.BlockSpec((tm, ATTN_HEAD_DIM), lambda i, j: (i % nseq, 0)),
            pl.BlockSpec((tm, ATTN_HEAD_DIM), lambda i, j: (i % nseq, 0)),
        ],
        out_specs=pl.BlockSpec((tm, tn), lambda i, j: (i, j)),
        out_shape=jax.ShapeDtypeStruct((m, n), BF16),
        scratch_shapes=[pltpu.VMEM((tm, k), BF16)],
        compiler_params=_params("parallel", "arbitrary"),
    )(x, g.reshape(1, k), w, gain, mult, cos_t, sin_t)


def _attn_kernel(q_ref, k_ref, v_ref, o_ref):
    hd = ATTN_HEAD_DIM
    k = k_ref[...]
    v = v_ref[...]
    for g in range(ATTN_GROUPS):
        sl = slice(g * hd, (g + 1) * hd)
        s = lax.dot_general(q_ref[:, sl], k, _NT, preferred_element_type=F32)
        p = jnp.exp(s - jnp.max(s, axis=-1, keepdims=True))
        l = jnp.sum(p, axis=-1, keepdims=True)
        o = jnp.dot(p.astype(BF16), v, preferred_element_type=F32)
        o_ref[:, sl] = (o / l).astype(o_ref.dtype)


def attn_core(qk, v, batch, seq, *, tq=512):
    hd, g = ATTN_HEAD_DIM, ATTN_GROUPS
    nq = seq // tq
    return pl.pallas_call(
        _attn_kernel,
        grid=(batch, ATTN_KV_HEADS, nq),
        in_specs=[
            pl.BlockSpec((tq, g * hd), lambda b, kv, i: (b * nq + i, kv)),
            pl.BlockSpec((seq, hd), lambda b, kv, i: (b, ATTN_Q_HEADS + kv)),
            pl.BlockSpec((seq, hd), lambda b, kv, i: (b, kv)),
        ],
        out_specs=pl.BlockSpec((tq, g * hd), lambda b, kv, i: (b * nq + i, kv)),
        out_shape=jax.ShapeDtypeStruct((batch * seq, ATTN_Q_HEADS * hd), BF16),
        compiler_params=_params("parallel", "parallel", "arbitrary"),
    )(qk, qk, v)


def _rope_tables(seq):
    rows = seq // GRID_W
    row = jnp.repeat(jnp.arange(rows, dtype=F32), GRID_W)
    col = jnp.tile(jnp.arange(GRID_W, dtype=F32), rows)
    n_freq = ATTN_HEAD_DIM // 4
    freqs = ROPE_THETA ** (-jnp.arange(n_freq, dtype=F32) / n_freq)
    ang = jnp.concatenate([row[:, None] * freqs, col[:, None] * freqs], axis=-1)
    cos, sin = jnp.cos(ang), jnp.sin(ang)
    return jnp.concatenate([cos, cos], axis=-1), jnp.concatenate([-sin, sin], axis=-1)


def attn_mixer(x2d, ln, w_in, q_norm, k_norm, w_out, batch, seq):
    hd = ATTN_HEAD_DIM
    n_qk_heads = ATTN_Q_HEADS + ATTN_KV_HEADS
    n_qk = n_qk_heads * hd
    perm = jnp.concatenate([jnp.arange(0, hd, 2), jnp.arange(1, hd, 2)])
    d_model = w_in.shape[0]
    w_qk = w_in[:, :n_qk].reshape(d_model, n_qk_heads, hd)[:, :, perm].reshape(d_model, n_qk).astype(BF16)
    w_v = w_in[:, n_qk:].astype(BF16)
    gain = jnp.concatenate([jnp.tile(q_norm[perm], ATTN_Q_HEADS), jnp.tile(k_norm[perm], ATTN_KV_HEADS)])
    mult = jnp.concatenate([jnp.full((ATTN_Q_HEADS * hd,), hd ** -0.5, F32),
                            jnp.ones((ATTN_KV_HEADS * hd,), F32)])
    cos_t, sin_t = _rope_tables(seq)
    qk = qk_proj(x2d, ln, w_qk, gain.reshape(1, n_qk).astype(F32), mult.reshape(1, n_qk), cos_t, sin_t, seq)
    v = norm_matmul(x2d, ln, w_v, BF16)
    o = attn_core(qk, v, batch, seq)
    return matmul_residual(o, w_out.astype(BF16), x2d)


_PEER_CAND = [(i, j) for i in range(PEER_TOPK) for j in range(min(PEER_TOPK, (PEER_TOPK + 1) // (i + 1)))]
_PEER_CAND_ROWS = -(-len(_PEER_CAND) // 8) * 8


def _top_values(x, n):
    vals = []
    for _ in range(n):
        m = jnp.max(x, axis=0, keepdims=True)
        vals.append(m)
        x = jnp.where(x == m, -jnp.inf, x)
    return vals


def _peer_route_kernel(q_ref, k1_ref, k2_ref, thr_ref, c_ref, e2_ref, s2_ref, cand_ref):
    nk, half, topk = PEER_N_KEYS, PEER_HALF, PEER_TOPK
    k1 = k1_ref[...]
    k2 = k2_ref[...]
    tm = q_ref.shape[0]
    cand_ref[...] = jnp.full(cand_ref.shape, -jnp.inf, F32)
    for h in range(PEER_HEADS):
        qa = q_ref[:, (2 * h) * half:(2 * h + 1) * half]
        qb = q_ref[:, (2 * h + 1) * half:(2 * h + 2) * half]
        s1 = lax.dot_general(k1, qa, _NT, preferred_element_type=F32)
        s2 = lax.dot_general(k2, qb, _NT, preferred_element_type=F32)
        t1 = _top_values(s1, topk + 1)
        t2 = _top_values(s2, topk + 1)
        for r, (i, j) in enumerate(_PEER_CAND):
            cand_ref[r:r + 1, :] = t1[i] + t2[j]
        c = _top_values(cand_ref[...], topk + 1)
        c_next = jnp.maximum(c[topk], jnp.maximum(t1[topk] + t2[0], t1[0] + t2[topk]))
        thr = 0.5 * (c[topk - 1] + c_next)
        z = jnp.zeros((1, tm), F32)
        for kk in range(topk):
            z = z + jnp.exp(c[kk] - c[0])
        rows = slice(h * nk, (h + 1) * nk)
        e2_ref[rows, :] = jnp.exp(s2 - t2[0])
        s2_ref[rows, :] = s2
        n_sub = thr_ref.shape[1] // PEER_HEADS
        thr_a = thr - s1
        c_a = jnp.exp(s1 - t1[0]) / z
        for blk in range(nk // n_sub):
            thr_ref[blk, h * n_sub:(h + 1) * n_sub, :] = thr_a[blk * n_sub:(blk + 1) * n_sub, :]
            c_ref[blk, h * n_sub:(h + 1) * n_sub, :] = c_a[blk * n_sub:(blk + 1) * n_sub, :]


def peer_route(q, k1, k2, n_sub, *, tm=512):
    t, d = q.shape
    rows = PEER_HEADS * PEER_N_KEYS
    blocks = PEER_N_KEYS // n_sub
    a_spec = pl.BlockSpec((blocks, PEER_HEADS * n_sub, tm), lambda i: (0, 0, i))
    a_shape = jax.ShapeDtypeStruct((blocks, PEER_HEADS * n_sub, t), F32)
    b_spec = pl.BlockSpec((rows, tm), lambda i: (0, i))
    b_shape = jax.ShapeDtypeStruct((rows, t), F32)
    return pl.pallas_call(
        _peer_route_kernel,
        grid=(t // tm,),
        in_specs=[
            pl.BlockSpec((tm, d), lambda i: (i, 0)),
            pl.BlockSpec((PEER_N_KEYS, PEER_HALF), lambda i: (0, 0)),
            pl.BlockSpec((PEER_N_KEYS, PEER_HALF), lambda i: (0, 0)),
        ],
        out_specs=[a_spec, a_spec, b_spec, b_spec],
        out_shape=[a_shape, a_shape, b_shape, b_shape],
        scratch_shapes=[pltpu.VMEM((_PEER_CAND_ROWS, tm), F32)],
        compiler_params=_params("parallel"),
    )(q, k1, k2)


def _gelu_tanh(x):
    return 0.5 * x * (1.0 + jnp.tanh(math.sqrt(2.0 / math.pi) * (x + 0.044715 * (x * x * x))))


def _peer_expert_kernel(xt_ref, u_ref, vt_ref, thr_ref, c_ref, e2_ref, s2_ref, res_ref, o_ref,
                        acc_ref, w0_ref, w1_ref, *, n_blocks):
    e = pl.program_id(1)
    nk = PEER_N_KEYS
    n_pieces = thr_ref.shape[0]
    pe = u_ref.shape[0] // n_pieces
    pd = vt_ref.shape[0] // n_pieces
    tt = xt_ref.shape[1]

    def make_weights(p, w_ref):
        e0 = pl.multiple_of(p * pe, pe)
        act = _gelu_tanh(jnp.dot(u_ref[pl.ds(e0, pe), :], xt_ref[...], preferred_element_type=F32))
        for a in range(pe // nk):
            for lc in range(tt // LANES):
                cols = slice(lc * LANES, (lc + 1) * LANES)
                gate = jnp.zeros((nk, LANES), F32)
                for h in range(PEER_HEADS):
                    r = h * (pe // nk) + a
                    rows = slice(h * nk, (h + 1) * nk)
                    gate = gate + jnp.where(s2_ref[rows, cols] >= thr_ref[p, r:r + 1, cols],
                                            e2_ref[rows, cols] * c_ref[p, r:r + 1, cols], 0.0)
                w_ref[pl.ds(e0 + a * nk, nk), cols] = (gate * act[a * nk:(a + 1) * nk, cols]).astype(BF16)

    def consume(p, w_ref):
        r0 = pl.multiple_of(p * pd, pd)
        acc_ref[pl.ds(r0, pd), :] += jnp.dot(vt_ref[pl.ds(r0, pd), :], w_ref[...], preferred_element_type=F32)

    def run(w_old_ref, w_new_ref):
        def body(p, carry):
            if w_old_ref is not None:
                consume(p, w_old_ref)
            if w_new_ref is not None:
                make_weights(p, w_new_ref)
            return carry
        lax.fori_loop(0, n_pieces, body, 0)

    @pl.when(e == 0)
    def _():
        acc_ref[...] = jnp.zeros_like(acc_ref)
        run(None, w0_ref)

    is_mid = jnp.logical_and(e > 0, e < n_blocks)

    @pl.when(jnp.logical_and(is_mid, e % 2 == 1))
    def _():
        run(w0_ref, w1_ref)

    @pl.when(jnp.logical_and(is_mid, e % 2 == 0))
    def _():
        run(w1_ref, w0_ref)

    @pl.when(e == n_blocks)
    def _():
        run(w1_ref if n_blocks % 2 == 0 else w0_ref, None)
        o_ref[...] = res_ref[...] + acc_ref[...].T


def peer_experts(xt, u, vt, thr, c, e2, s2, res, *, tt=512):
    d, t = xt.shape
    n_exp = u.shape[0]
    eb = PEER_EXPERT_BLOCK
    n_blocks = n_exp // eb
    n_pieces = eb // PEER_EXPERT_PIECE
    rows = PEER_HEADS * PEER_N_KEYS
    once = pl.Buffered(1)
    side_a = pl.BlockSpec((n_pieces, thr.shape[1], tt), lambda i, e: (jnp.minimum(e, n_blocks - 1), 0, i))
    side_b = pl.BlockSpec((rows, tt), lambda i, e: (0, i), pipeline_mode=once)
    kernel = functools.partial(_peer_expert_kernel, n_blocks=n_blocks)
    return pl.pallas_call(
        kernel,
        grid=(t // tt, n_blocks + 1),
        in_specs=[
            pl.BlockSpec((d, tt), lambda i, e: (0, i), pipeline_mode=once),
            pl.BlockSpec((eb, d), lambda i, e: (jnp.minimum(e, n_blocks - 1), 0)),
            pl.BlockSpec((d, eb), lambda i, e: (0, jnp.maximum(e - 1, 0))),
            side_a, side_a, side_b, side_b,
            pl.BlockSpec((tt, d), lambda i, e: (i, 0), pipeline_mode=once),
        ],
        out_specs=pl.BlockSpec((tt, d), lambda i, e: (i, 0)),
        out_shape=jax.ShapeDtypeStruct((t, d), F32),
        scratch_shapes=[pltpu.VMEM((d, tt), F32), pltpu.VMEM((eb, tt), BF16), pltpu.VMEM((eb, tt), BF16)],
        compiler_params=_params("parallel", "arbitrary"),
    )(xt, u, vt, thr, c, e2, s2, res)


def peer_mixer(x2d, ln, wq, k1, k2, u, v):
    q, xt = norm_matmul(x2d, ln, wq.astype(BF16), BF16, with_transposed_norm=True)
    thr, c, e2, s2 = peer_route(q, k1.astype(BF16), k2.astype(BF16), PEER_EXPERT_PIECE // PEER_N_KEYS)
    return peer_experts(xt, u.astype(BF16), v.T.astype(BF16), thr, c, e2, s2, x2d)


def kernel(x, ln_mix, ln_ffn, mlstm_w_in, mlstm_gate_bias, mlstm_head_norm, mlstm_w_out,
           attn_w_in, attn_q_norm, attn_k_norm, attn_w_out,
           peer_wq, peer_k1, peer_k2, peer_u, peer_v):
    batch, seq, d_model = x.shape
    depth = ln_mix.shape[0]
    x2d = x.reshape(batch * seq, d_model)
    for layer in range(depth):
        j = layer // 2
        if layer % 2 == 0:
            x2d = mlstm_mixer(x2d, ln_mix[layer], mlstm_w_in[j], mlstm_gate_bias[j], mlstm_head_norm[j],
                              mlstm_w_out[j], batch, seq)
        else:
            x2d = attn_mixer(x2d, ln_mix[layer], attn_w_in[j], attn_q_norm[j], attn_k_norm[j],
                             attn_w_out[j], batch, seq)
        x2d = peer_mixer(x2d, ln_ffn[layer], peer_wq[layer], peer_k1[layer], peer_k2[layer],
                         peer_u[layer], peer_v[layer])
    return x2d.reshape(batch, seq, d_model)
```

```python
import functools
import math

import jax
import jax.numpy as jnp
from jax import lax
from jax.experimental import pallas as pl
from jax.experimental.pallas import tpu as pltpu

NORM_EPS = 1e-6

MLSTM_HEADS = 4
MLSTM_QK_DIM = 256
MLSTM_V_DIM = 512

ATTN_HEAD_DIM = 128
ATTN_Q_HEADS = 16
ATTN_KV_HEADS = 4
ATTN_GROUPS = ATTN_Q_HEADS // ATTN_KV_HEADS
GRID_W = 64
ROPE_THETA = 10000.0

PEER_HEADS = 8
PEER_N_KEYS = 128
PEER_TOPK = 16
PEER_HALF = 128
PEER_EXPERT_BLOCK = 1024
PEER_OUT_CHUNK = 256
PEER_ACT_CHUNK = 256

LANES = 128
VMEM_LIMIT_BYTES = 56 * 1024 * 1024

BF16 = jnp.bfloat16
F32 = jnp.float32

_NT = (((1,), (1,)), ((), ()))


def _params(*sem):
    return pltpu.CompilerParams(dimension_semantics=sem, vmem_limit_bytes=VMEM_LIMIT_BYTES)


def _rms(x, g):
    return x * lax.rsqrt(jnp.mean(x * x, axis=-1, keepdims=True) + NORM_EPS) * g


def _norm_matmul_kernel(x_ref, g_ref, w_ref, o_ref, xn_ref):
    @pl.when(pl.program_id(1) == 0)
    def _():
        xn_ref[...] = _rms(x_ref[...], g_ref[...]).astype(BF16)

    o_ref[...] = jnp.dot(xn_ref[...], w_ref[...], preferred_element_type=F32).astype(o_ref.dtype)


def _norm_matmul_t_kernel(x_ref, g_ref, w_ref, o_ref, xt_ref, xn_ref):
    @pl.when(pl.program_id(1) == 0)
    def _():
        y = _rms(x_ref[...], g_ref[...])
        xn_ref[...] = y.astype(BF16)
        xt_ref[...] = y.T.astype(BF16)

    o_ref[...] = jnp.dot(xn_ref[...], w_ref[...], preferred_element_type=F32).astype(o_ref.dtype)


def norm_matmul(x, g, w, out_dtype, *, tm=512, tn=512, with_transposed_norm=False):
    m, k = x.shape
    n = w.shape[1]
    tn = min(tn, n)
    grid = (m // tm, n // tn)
    in_specs = [
        pl.BlockSpec((tm, k), lambda i, j: (i, 0)),
        pl.BlockSpec((1, k), lambda i, j: (0, 0)),
        pl.BlockSpec((k, tn), lambda i, j: (0, j)),
    ]
    o_spec = pl.BlockSpec((tm, tn), lambda i, j: (i, j))
    scratch = [pltpu.VMEM((tm, k), BF16)]
    if with_transposed_norm:
        return pl.pallas_call(
            _norm_matmul_t_kernel,
            grid=grid, in_specs=in_specs,
            out_specs=[o_spec, pl.BlockSpec((k, tm), lambda i, j: (0, i))],
            out_shape=[jax.ShapeDtypeStruct((m, n), out_dtype), jax.ShapeDtypeStruct((k, m), BF16)],
            scratch_shapes=scratch,
            compiler_params=_params("parallel", "arbitrary"),
        )(x, g.reshape(1, k), w)
    return pl.pallas_call(
        _norm_matmul_kernel,
        grid=grid, in_specs=in_specs, out_specs=o_spec,
        out_shape=jax.ShapeDtypeStruct((m, n), out_dtype),
        scratch_shapes=scratch,
        compiler_params=_params("parallel", "arbitrary"),
    )(x, g.reshape(1, k), w)


def _matmul_residual_kernel(x_ref, w_ref, r_ref, o_ref):
    o_ref[...] = r_ref[...] + jnp.dot(x_ref[...], w_ref[...], preferred_element_type=F32)


def matmul_residual(x, w, res, *, tm=512, tn=512):
    m, k = x.shape
    n = w.shape[1]
    return pl.pallas_call(
        _matmul_residual_kernel,
        grid=(m // tm, n // tn),
        in_specs=[
            pl.BlockSpec((tm, k), lambda i, j: (i, 0)),
            pl.BlockSpec((k, tn), lambda i, j: (0, j)),
            pl.BlockSpec((tm, tn), lambda i, j: (i, j)),
        ],
        out_specs=pl.BlockSpec((tm, tn), lambda i, j: (i, j)),
        out_shape=jax.ShapeDtypeStruct((m, n), F32),
        compiler_params=_params("parallel", "arbitrary"),
    )(x, w, res)


def _scan_lanes(x, op, fill, is_suffix):
    s = x.shape[-1]
    lane = lax.broadcasted_iota(jnp.int32, x.shape, 1)
    d = 1
    while d < s:
        from_left = jnp.where(lane >= d, pltpu.roll(x, d, axis=1), fill)
        from_right = jnp.where(lane < s - d, pltpu.roll(x, s - d, axis=1), fill)
        x = op(x, jnp.where(is_suffix, from_right, from_left))
        d *= 2
    return x


def _gate_scan_kernel(ig_ref, fg_ref, bi_ref, bf_ref, a_ref, m_ref, e_ref):
    ig = ig_ref[0] + bi_ref[...]
    fg = fg_ref[0] + bf_ref[...]
    is_suffix = lax.broadcasted_iota(jnp.int32, ig.shape, 0) >= MLSTM_HEADS
    logsig = jnp.minimum(fg, 0.0) - jnp.log1p(jnp.exp(-jnp.abs(fg)))
    fcum = _scan_lanes(logsig, jnp.add, 0.0, is_suffix)
    a = ig - fcum
    mx = jnp.maximum(_scan_lanes(a, jnp.maximum, -jnp.inf, is_suffix), 0.0)
    a_ref[0] = a
    m_ref[0] = mx
    e_ref[0] = jnp.exp(-(fcum + mx))


def gate_scan(ig, fg, bi, bf):
    b, r, s = ig.shape
    g_spec = pl.BlockSpec((1, r, s), lambda i: (i, 0, 0))
    b_spec = pl.BlockSpec((r, 1), lambda i: (0, 0))
    o_shape = jax.ShapeDtypeStruct((b, r, s), F32)
    return pl.pallas_call(
        _gate_scan_kernel,
        grid=(b,),
        in_specs=[g_spec, g_spec, b_spec, b_spec],
        out_specs=[g_spec] * 3,
        out_shape=[o_shape] * 3,
        compiler_params=_params("parallel"),
    )(ig, fg, bi, bf)


def _mlstm_kernel(q_ref, k_ref, v_ref, o_ref, row_ref, col_ref, hn_ref, out_ref,
                  numf_ref, numb_ref, denf_ref, denb_ref, *, tq, nk):
    i = pl.program_id(2)
    q = q_ref[...]
    col = col_ref[0, 0]
    m_f, m_b, e_f, e_b = col[:, 0:1], col[:, 1:2], col[:, 2:3], col[:, 3:4]
    scale = MLSTM_QK_DIM ** -0.5

    numf_ref[...] = jnp.zeros_like(numf_ref)
    numb_ref[...] = jnp.zeros_like(numb_ref)
    denf_ref[...] = jnp.zeros_like(denf_ref)
    denb_ref[...] = jnp.zeros_like(denb_ref)

    def scores(j):
        start = pl.multiple_of(j * tq, tq)
        kj = k_ref[pl.ds(start, tq), :]
        vj = v_ref[pl.ds(start, tq), :]
        return lax.dot_general(q, kj, _NT, preferred_element_type=F32) * scale, vj

    def accumulate(s, vj, num_ref, den_ref):
        den_ref[...] += jnp.sum(s, axis=-1, keepdims=True)
        num_ref[...] += jnp.dot(s.astype(BF16), vj, preferred_element_type=F32)

    def fwd_block(j, carry):
        s, vj = scores(j)
        a = row_ref[0, 0, j][0:1, :]
        accumulate(s * jnp.exp(a - m_f), vj, numf_ref, denf_ref)
        return carry

    def bwd_block(j, carry):
        s, vj = scores(j)
        a = row_ref[0, 0, j][1:2, :]
        accumulate(s * jnp.exp(a - m_b), vj, numb_ref, denb_ref)
        return carry

    lax.fori_loop(0, i, fwd_block, 0)
    lax.fori_loop(i + 1, nk, bwd_block, 0)

    s, vj = scores(i)
    rows = row_ref[0, 0, i]
    t_idx = lax.broadcasted_iota(jnp.int32, (tq, tq), 0)
    s_idx = lax.broadcasted_iota(jnp.int32, (tq, tq), 1)
    d_f = jnp.where(s_idx <= t_idx, jnp.exp(rows[0:1, :] - m_f), 0.0)
    d_b = jnp.where(s_idx >= t_idx, jnp.exp(rows[1:2, :] - m_b), 0.0)
    accumulate(s * d_f, vj, numf_ref, denf_ref)
    accumulate(s * d_b, vj, numb_ref, denb_ref)

    h = (numf_ref[...] / jnp.maximum(jnp.abs(denf_ref[...]), e_f)
         + numb_ref[...] / jnp.maximum(jnp.abs(denb_ref[...]), e_b))
    h = _rms(h, hn_ref[0])
    out_ref[...] = (h * jax.nn.sigmoid(o_ref[...].astype(F32))).astype(out_ref.dtype)


def mlstm_core(proj, rowp, colp, head_norm, batch, seq, *, tq=256):
    h, dk, dv = MLSTM_HEADS, MLSTM_QK_DIM, MLSTM_V_DIM
    nq = seq // tq
    k_col0 = h * dk // dk
    v_col0 = 2 * h * dk // dv
    o_col0 = (2 * h * dk + h * dv) // dv
    kernel = functools.partial(_mlstm_kernel, tq=tq, nk=nq)
    return pl.pallas_call(
        kernel,
        grid=(batch, h, nq),
        in_specs=[
            pl.BlockSpec((tq, dk), lambda b, hh, i: (b * nq + i, hh)),
            pl.BlockSpec((seq, dk), lambda b, hh, i: (b, k_col0 + hh)),
            pl.BlockSpec((seq, dv), lambda b, hh, i: (b, v_col0 + hh)),
            pl.BlockSpec((tq, dv), lambda b, hh, i: (b * nq + i, o_col0 + hh)),
            pl.BlockSpec((1, 1, nq, 8, tq), lambda b, hh, i: (b, hh, 0, 0, 0)),
            pl.BlockSpec((1, 1, tq, 8), lambda b, hh, i: (b, hh, i, 0)),
            pl.BlockSpec((1, 1, dv), lambda b, hh, i: (hh, 0, 0)),
        ],
        out_specs=pl.BlockSpec((tq, dv), lambda b, hh, i: (b * nq + i, hh)),
        out_shape=jax.ShapeDtypeStruct((batch * seq, h * dv), BF16),
        scratch_shapes=[pltpu.VMEM((tq, dv), F32), pltpu.VMEM((tq, dv), F32),
                        pltpu.VMEM((tq, 1), F32), pltpu.VMEM((tq, 1), F32)],
        compiler_params=_params("parallel", "parallel", "arbitrary"),
    )(proj, proj, proj, proj, rowp, colp, head_norm.reshape(h, 1, dv))


def mlstm_mixer(x2d, ln, w_in, gate_bias, head_norm, w_out, batch, seq):
    h = MLSTM_HEADS
    n_main = 2 * h * MLSTM_QK_DIM + 2 * h * MLSTM_V_DIM
    w_main = w_in[:, :n_main].astype(BF16)
    w_gate = jnp.pad(w_in[:, n_main:], ((0, 0), (0, LANES - 4 * h))).astype(BF16)
    proj = norm_matmul(x2d, ln, w_main, BF16, tn=1024)
    gates = norm_matmul(x2d, ln, w_gate, F32)[:, :4 * h]
    gates_t = gates.reshape(batch, seq, 4, h).transpose(0, 2, 3, 1)
    bias = gate_bias.astype(F32)
    a, mx, en = gate_scan(gates_t[:, 0::2].reshape(batch, 2 * h, seq), gates_t[:, 1::2].reshape(batch, 2 * h, seq),
                          bias[0::2].reshape(2 * h, 1), bias[1::2].reshape(2 * h, 1))
    tq = 256
    nk = seq // tq
    rowp = a.reshape(batch, 2, h, nk, tq).transpose(0, 2, 3, 1, 4)
    rowp = jnp.pad(rowp, ((0, 0), (0, 0), (0, 0), (0, 6), (0, 0)))
    colp = jnp.concatenate([mx.reshape(batch, 2, h, seq), en.reshape(batch, 2, h, seq)], axis=1)
    colp = jnp.pad(colp.transpose(0, 2, 3, 1), ((0, 0), (0, 0), (0, 0), (0, 4)))
    hg = mlstm_core(proj, rowp, colp, head_norm, batch, seq, tq=tq)
    return matmul_residual(hg, w_out.astype(BF16), x2d)


def _qk_proj_kernel(x_ref, g_ref, w_ref, gain_ref, mult_ref, cos_ref, sin_ref, o_ref, xn_ref):
    @pl.when(pl.program_id(1) == 0)
    def _():
        xn_ref[...] = _rms(x_ref[...], g_ref[...]).astype(BF16)

    acc = jnp.dot(xn_ref[...], w_ref[...], preferred_element_type=F32)
    hd = ATTN_HEAD_DIM
    cos = cos_ref[...]
    sin = sin_ref[...]
    for c in range(acc.shape[1] // hd):
        sl = slice(c * hd, (c + 1) * hd)
        y = _rms(acc[:, sl], gain_ref[:, sl])
        y = y * cos + pltpu.roll(y, hd // 2, axis=1) * sin
        o_ref[:, sl] = (y * mult_ref[:, sl]).astype(o_ref.dtype)


def qk_proj(x, g, w, gain, mult, cos_t, sin_t, seq, *, tm=512, tn=512):
    m, k = x.shape
    n = w.shape[1]
    nseq = seq // tm
    return pl.pallas_call(
        _qk_proj_kernel,
        grid=(m // tm, n // tn),
        in_specs=[
            pl.BlockSpec((tm, k), lambda i, j: (i, 0)),
            pl.BlockSpec((1, k), lambda i, j: (0, 0)),
            pl.BlockSpec((k, tn), lambda i, j: (0, j)),
            pl.BlockSpec((1, tn), lambda i, j: (0, j)),
            pl.BlockSpec((1, tn), lambda i, j: (0, j)),
            pl.BlockSpec((tm, ATTN_HEAD_DIM), lambda i, j: (i % nseq, 0)),
            pl.BlockSpec((tm, ATTN_HEAD_DIM), lambda i, j: (i % nseq, 0)),
        ],
        out_specs=pl.BlockSpec((tm, tn), lambda i, j: (i, j)),
        out_shape=jax.ShapeDtypeStruct((m, n), BF16),
        scratch_shapes=[pltpu.VMEM((tm, k), BF16)],
        compiler_params=_params("parallel", "arbitrary"),
    )(x, g.reshape(1, k), w, gain, mult, cos_t, sin_t)


def _attn_kernel(q_ref, k_ref, v_ref, o_ref):
    hd = ATTN_HEAD_DIM
    k = k_ref[...]
    v = v_ref[...]
    for g in range(ATTN_GROUPS):
        sl = slice(g * hd, (g + 1) * hd)
        s = lax.dot_general(q_ref[:, sl], k, _NT, preferred_element_type=F32)
        p = jnp.exp(s - jnp.max(s, axis=-1, keepdims=True))
        l = jnp.sum(p, axis=-1, keepdims=True)
        o = jnp.dot(p.astype(BF16), v, preferred_element_type=F32)
        o_ref[:, sl] = (o / l).astype(o_ref.dtype)


def attn_core(qk, v, batch, seq, *, tq=512):
    hd, g = ATTN_HEAD_DIM, ATTN_GROUPS
    nq = seq // tq
    return pl.pallas_call(
        _attn_kernel,
        grid=(batch, ATTN_KV_HEADS, nq),
        in_specs=[
            pl.BlockSpec((tq, g * hd), lambda b, kv, i: (b * nq + i, kv)),
            pl.BlockSpec((seq, hd), lambda b, kv, i: (b, ATTN_Q_HEADS + kv)),
            pl.BlockSpec((seq, hd), lambda b, kv, i: (b, kv)),
        ],
        out_specs=pl.BlockSpec((tq, g * hd), lambda b, kv, i: (b * nq + i, kv)),
        out_shape=jax.ShapeDtypeStruct((batch * seq, ATTN_Q_HEADS * hd), BF16),
        compiler_params=_params("parallel", "parallel", "arbitrary"),
    )(qk, qk, v)


def _rope_tables(seq):
    rows = seq // GRID_W
    row = jnp.repeat(jnp.arange(rows, dtype=F32), GRID_W)
    col = jnp.tile(jnp.arange(GRID_W, dtype=F32), rows)
    n_freq = ATTN_HEAD_DIM // 4
    freqs = ROPE_THETA ** (-jnp.arange(n_freq, dtype=F32) / n_freq)
    ang = jnp.concatenate([row[:, None] * freqs, col[:, None] * freqs], axis=-1)
    cos, sin = jnp.cos(ang), jnp.sin(ang)
    return jnp.concatenate([cos, cos], axis=-1), jnp.concatenate([-sin, sin], axis=-1)


def attn_mixer(x2d, ln, w_in, q_norm, k_norm, w_out, batch, seq):
    hd = ATTN_HEAD_DIM
    n_qk_heads = ATTN_Q_HEADS + ATTN_KV_HEADS
    n_qk = n_qk_heads * hd
    perm = jnp.concatenate([jnp.arange(0, hd, 2), jnp.arange(1, hd, 2)])
    d_model = w_in.shape[0]
    w_qk = w_in[:, :n_qk].reshape(d_model, n_qk_heads, hd)[:, :, perm].reshape(d_model, n_qk).astype(BF16)
    w_v = w_in[:, n_qk:].astype(BF16)
    gain = jnp.concatenate([jnp.tile(q_norm[perm], ATTN_Q_HEADS), jnp.tile(k_norm[perm], ATTN_KV_HEADS)])
    mult = jnp.concatenate([jnp.full((ATTN_Q_HEADS * hd,), hd ** -0.5, F32),
                            jnp.ones((ATTN_KV_HEADS * hd,), F32)])
    cos_t, sin_t = _rope_tables(seq)
    qk = qk_proj(x2d, ln, w_qk, gain.reshape(1, n_qk).astype(F32), mult.reshape(1, n_qk), cos_t, sin_t, seq)
    v = norm_matmul(x2d, ln, w_v, BF16)
    o = attn_core(qk, v, batch, seq)
    return matmul_residual(o, w_out.astype(BF16), x2d)


_PEER_CAND = [(i, j) for i in range(PEER_TOPK) for j in range(min(PEER_TOPK, (PEER_TOPK + 1) // (i + 1)))]
_PEER_CAND_ROWS = -(-len(_PEER_CAND) // 8) * 8


def _top_values(x, n):
    vals = []
    for _ in range(n):
        m = jnp.max(x, axis=0, keepdims=True)
        vals.append(m)
        x = jnp.where(x == m, -jnp.inf, x)
    return vals


def _peer_route_kernel(q_ref, k1_ref, k2_ref, thr_ref, c_ref, e2_ref, s2_ref, cand_ref):
    nk, half, topk = PEER_N_KEYS, PEER_HALF, PEER_TOPK
    k1 = k1_ref[...]
    k2 = k2_ref[...]
    tm = q_ref.shape[0]
    cand_ref[...] = jnp.full(cand_ref.shape, -jnp.inf, F32)
    for h in range(PEER_HEADS):
        qa = q_ref[:, (2 * h) * half:(2 * h + 1) * half]
        qb = q_ref[:, (2 * h + 1) * half:(2 * h + 2) * half]
        s1 = lax.dot_general(k1, qa, _NT, preferred_element_type=F32)
        s2 = lax.dot_general(k2, qb, _NT, preferred_element_type=F32)
        t1 = _top_values(s1, topk + 1)
        t2 = _top_values(s2, topk + 1)
        for r, (i, j) in enumerate(_PEER_CAND):
            cand_ref[r:r + 1, :] = t1[i] + t2[j]
        c = _top_values(cand_ref[...], topk + 1)
        c_next = jnp.maximum(c[topk], jnp.maximum(t1[topk] + t2[0], t1[0] + t2[topk]))
        thr = 0.5 * (c[topk - 1] + c_next)
        z = jnp.zeros((1, tm), F32)
        for kk in range(topk):
            z = z + jnp.exp(c[kk] - c[0])
        rows = slice(h * nk, (h + 1) * nk)
        e2 = jnp.exp(s2 - t2[0])
        for lc in range(tm // LANES):
            cols = slice(lc * LANES, (lc + 1) * LANES)
            e2_ref[lc, rows, :] = e2[:, cols]
            s2_ref[lc, rows, :] = s2[:, cols]
        n_sub = thr_ref.shape[1] // PEER_HEADS
        thr_a = thr - s1
        c_a = jnp.exp(s1 - t1[0]) / z
        for blk in range(nk // n_sub):
            thr_ref[blk, h * n_sub:(h + 1) * n_sub, :] = thr_a[blk * n_sub:(blk + 1) * n_sub, :]
            c_ref[blk, h * n_sub:(h + 1) * n_sub, :] = c_a[blk * n_sub:(blk + 1) * n_sub, :]


def peer_route(q, k1, k2, n_sub, *, tm=512):
    t, d = q.shape
    rows = PEER_HEADS * PEER_N_KEYS
    blocks = PEER_N_KEYS // n_sub
    a_spec = pl.BlockSpec((blocks, PEER_HEADS * n_sub, tm), lambda i: (0, 0, i))
    a_shape = jax.ShapeDtypeStruct((blocks, PEER_HEADS * n_sub, t), F32)
    b_spec = pl.BlockSpec((tm // LANES, rows, LANES), lambda i: (i, 0, 0))
    b_shape = jax.ShapeDtypeStruct((t // LANES, rows, LANES), F32)
    return pl.pallas_call(
        _peer_route_kernel,
        grid=(t // tm,),
        in_specs=[
            pl.BlockSpec((tm, d), lambda i: (i, 0)),
            pl.BlockSpec((PEER_N_KEYS, PEER_HALF), lambda i: (0, 0)),
            pl.BlockSpec((PEER_N_KEYS, PEER_HALF), lambda i: (0, 0)),
        ],
        out_specs=[a_spec, a_spec, b_spec, b_spec],
        out_shape=[a_shape, a_shape, b_shape, b_shape],
        scratch_shapes=[pltpu.VMEM((_PEER_CAND_ROWS, tm), F32)],
        compiler_params=_params("parallel"),
    )(q, k1, k2)


def _gelu_tanh(x):
    return 0.5 * x * (1.0 + jnp.tanh(math.sqrt(2.0 / math.pi) * (x + 0.044715 * (x * x * x))))


def _peer_expert_kernel(xt_ref, u_ref, vt_ref, thr_ref, c_ref, e2_ref, s2_ref, res_ref, zero_ref, o_ref,
                        acc_ref, w0_ref, w1_ref, g_ref, *, n_blocks):
    e = pl.program_id(1)
    nk = PEER_N_KEYS
    eb, d = u_ref.shape
    tt = xt_ref.shape[1]
    n_sub = eb // nk
    lane_tiles = tt // LANES
    n_stages = eb // PEER_ACT_CHUNK
    keys_per_stage = PEER_ACT_CHUNK // nk
    out_per_stage = d // PEER_OUT_CHUNK // n_stages

    def gate_tiles(a):
        zeros = []
        for lc in range(lane_tiles):
            cols = slice(lc * LANES, (lc + 1) * LANES)
            gate = jnp.zeros((nk, LANES), F32)
            for h in range(PEER_HEADS):
                r = h * n_sub + a
                rows = slice(h * nk, (h + 1) * nk)
                gate = gate + jnp.where(s2_ref[lc, rows, :] >= thr_ref[0, r:r + 1, cols],
                                        e2_ref[lc, rows, :] * c_ref[0, r:r + 1, cols], 0.0)
            g_ref[a * nk:(a + 1) * nk, cols] = gate
            folded = gate[0:8, :]
            for r in range(8, nk, 8):
                folded = folded + gate[r:r + 8, :]
            zeros.append(pltpu.bitcast(folded, jnp.uint32) & zero_ref[...])
        return zeros

    def paced(lhs, zeros):
        if not zeros:
            return lhs
        z = functools.reduce(jnp.bitwise_or, zeros)
        words = pltpu.bitcast(lhs, jnp.uint32)
        words = words | jnp.tile(z, (words.shape[0] // z.shape[0], words.shape[1] // z.shape[1]))
        return pltpu.bitcast(words, lhs.dtype)

    def run(w_old_ref, w_new_ref):
        for s in range(n_stages):
            zeros = []
            if w_new_ref is not None:
                for a in range(s * keys_per_stage, (s + 1) * keys_per_stage):
                    zeros += gate_tiles(a)
            if w_old_ref is not None:
                for k in range(s * out_per_stage, (s + 1) * out_per_stage):
                    rows = slice(k * PEER_OUT_CHUNK, (k + 1) * PEER_OUT_CHUNK)
                    share = len(zeros) // out_per_stage
                    lhs = paced(vt_ref[rows, :], zeros[(k % out_per_stage) * share:(k % out_per_stage + 1) * share])
                    acc_ref[rows, :] += jnp.dot(lhs, w_old_ref[...], preferred_element_type=F32)
            if w_new_ref is not None:
                rows = slice(s * PEER_ACT_CHUNK, (s + 1) * PEER_ACT_CHUNK)
                act = _gelu_tanh(jnp.dot(paced(u_ref[rows, :], zeros), xt_ref[...], preferred_element_type=F32))
                w_new_ref[rows, :] = (g_ref[rows, :] * act).astype(BF16)

    @pl.when(e == 0)
    def _():
        acc_ref[...] = jnp.zeros_like(acc_ref)
        run(None, w0_ref)

    is_mid = jnp.logical_and(e > 0, e < n_blocks)

    @pl.when(jnp.logical_and(is_mid, e % 2 == 1))
    def _():
        run(w0_ref, w1_ref)

    @pl.when(jnp.logical_and(is_mid, e % 2 == 0))
    def _():
        run(w1_ref, w0_ref)

    @pl.when(e == n_blocks)
    def _():
        run(w1_ref if n_blocks % 2 == 0 else w0_ref, None)
        o_ref[...] = res_ref[...] + acc_ref[...].T


def peer_experts(xt, u, vt, thr, c, e2, s2, res, *, tt=512):
    d, t = xt.shape
    n_exp = u.shape[0]
    eb = PEER_EXPERT_BLOCK
    n_blocks = n_exp // eb
    rows = PEER_HEADS * PEER_N_KEYS
    once = pl.Buffered(1)
    side_a = pl.BlockSpec((1, thr.shape[1], tt), lambda i, e: (jnp.minimum(e, n_blocks - 1), 0, i))
    side_b = pl.BlockSpec((tt // LANES, rows, LANES), lambda i, e: (i, 0, 0), pipeline_mode=once)
    kernel = functools.partial(_peer_expert_kernel, n_blocks=n_blocks)
    return pl.pallas_call(
        kernel,
        grid=(t // tt, n_blocks + 1),
        in_specs=[
            pl.BlockSpec((d, tt), lambda i, e: (0, i), pipeline_mode=once),
            pl.BlockSpec((eb, d), lambda i, e: (jnp.minimum(e, n_blocks - 1), 0)),
            pl.BlockSpec((d, eb), lambda i, e: (0, jnp.maximum(e - 1, 0))),
            side_a, side_a, side_b, side_b,
            pl.BlockSpec((tt, d), lambda i, e: (i, 0), pipeline_mode=once),
            pl.BlockSpec((8, LANES), lambda i, e: (0, 0)),
        ],
        out_specs=pl.BlockSpec((tt, d), lambda i, e: (i, 0)),
        out_shape=jax.ShapeDtypeStruct((t, d), F32),
        scratch_shapes=[pltpu.VMEM((d, tt), F32), pltpu.VMEM((eb, tt), BF16), pltpu.VMEM((eb, tt), BF16),
                        pltpu.VMEM((eb, tt), F32)],
        compiler_params=_params("parallel", "arbitrary"),
    )(xt, u, vt, thr, c, e2, s2, res, jnp.zeros((8, LANES), jnp.uint32))


def peer_mixer(x2d, ln, wq, k1, k2, u, v):
    q, xt = norm_matmul(x2d, ln, wq.astype(BF16), BF16, with_transposed_norm=True)
    thr, c, e2, s2 = peer_route(q, k1.astype(BF16), k2.astype(BF16), PEER_EXPERT_BLOCK // PEER_N_KEYS)
    return peer_experts(xt, u.astype(BF16), v.T.astype(BF16), thr, c, e2, s2, x2d)


def kernel(x, ln_mix, ln_ffn, mlstm_w_in, mlstm_gate_bias, mlstm_head_norm, mlstm_w_out,
           attn_w_in, attn_q_norm, attn_k_norm, attn_w_out,
           peer_wq, peer_k1, peer_k2, peer_u, peer_v):
    batch, seq, d_model = x.shape
    depth = ln_mix.shape[0]
    x2d = x.reshape(batch * seq, d_model)
    for layer in range(depth):
        j = layer // 2
        if layer % 2 == 0:
            x2d = mlstm_mixer(x2d, ln_mix[layer], mlstm_w_in[j], mlstm_gate_bias[j], mlstm_head_norm[j],
                              mlstm_w_out[j], batch, seq)
        else:
            x2d = attn_mixer(x2d, ln_mix[layer], attn_w_in[j], attn_q_norm[j], attn_k_norm[j],
                             attn_w_out[j], batch, seq)
        x2d = peer_mixer(x2d, ln_ffn[layer], peer_wq[layer], peer_k1[layer], peer_k2[layer],
                         peer_u[layer], peer_v[layer])
    return x2d.reshape(batch, seq, d_model)
```

```python
import functools
import math

import jax
import jax.numpy as jnp
from jax import lax
from jax.experimental import pallas as pl
from jax.experimental.pallas import tpu as pltpu

NORM_EPS = 1e-6

MLSTM_HEADS = 4
MLSTM_QK_DIM = 256
MLSTM_V_DIM = 512
MLSTM_BLOCK = 512

ATTN_HEAD_DIM = 128
ATTN_Q_HEADS = 16
ATTN_KV_HEADS = 4
ATTN_GROUPS = ATTN_Q_HEADS // ATTN_KV_HEADS
GRID_W = 64
ROPE_THETA = 10000.0

PEER_HEADS = 8
PEER_N_KEYS = 128
PEER_TOPK = 16
PEER_HALF = 128
PEER_EXPERT_BLOCK = 1024
PEER_OUT_CHUNK = 256
PEER_ACT_CHUNK = 256
PEER_GATE_ROWS = 32

LANES = 128
VMEM_LIMIT_BYTES = 56 * 1024 * 1024

BF16 = jnp.bfloat16
F32 = jnp.float32

_NT = (((1,), (1,)), ((), ()))


def _params(*sem):
    return pltpu.CompilerParams(dimension_semantics=sem, vmem_limit_bytes=VMEM_LIMIT_BYTES)


def _rms(x, g):
    return x * lax.rsqrt(jnp.mean(x * x, axis=-1, keepdims=True) + NORM_EPS) * g


def _norm_matmul_kernel(x_ref, g_ref, w_ref, o_ref, xn_ref):
    @pl.when(pl.program_id(1) == 0)
    def _():
        xn_ref[...] = _rms(x_ref[...], g_ref[...]).astype(BF16)

    o_ref[...] = jnp.dot(xn_ref[...], w_ref[...], preferred_element_type=F32).astype(o_ref.dtype)


def _norm_matmul_t_kernel(x_ref, g_ref, w_ref, o_ref, xt_ref, xn_ref):
    @pl.when(pl.program_id(1) == 0)
    def _():
        y = _rms(x_ref[...], g_ref[...])
        xn_ref[...] = y.astype(BF16)
        xt_ref[...] = y.T.astype(BF16)

    o_ref[...] = jnp.dot(xn_ref[...], w_ref[...], preferred_element_type=F32).astype(o_ref.dtype)


def norm_matmul(x, g, w, out_dtype, *, tm=1024, tn=1024, with_transposed_norm=False):
    m, k = x.shape
    n = w.shape[1]
    tn = min(tn, n)
    grid = (m // tm, n // tn)
    in_specs = [
        pl.BlockSpec((tm, k), lambda i, j: (i, 0)),
        pl.BlockSpec((1, k), lambda i, j: (0, 0)),
        pl.BlockSpec((k, tn), lambda i, j: (0, j)),
    ]
    o_spec = pl.BlockSpec((tm, tn), lambda i, j: (i, j))
    scratch = [pltpu.VMEM((tm, k), BF16)]
    if with_transposed_norm:
        return pl.pallas_call(
            _norm_matmul_t_kernel,
            grid=grid, in_specs=in_specs,
            out_specs=[o_spec, pl.BlockSpec((k, tm), lambda i, j: (0, i))],
            out_shape=[jax.ShapeDtypeStruct((m, n), out_dtype), jax.ShapeDtypeStruct((k, m), BF16)],
            scratch_shapes=scratch,
            compiler_params=_params("parallel", "arbitrary"),
        )(x, g.reshape(1, k), w)
    return pl.pallas_call(
        _norm_matmul_kernel,
        grid=grid, in_specs=in_specs, out_specs=o_spec,
        out_shape=jax.ShapeDtypeStruct((m, n), out_dtype),
        scratch_shapes=scratch,
        compiler_params=_params("parallel", "arbitrary"),
    )(x, g.reshape(1, k), w)


def _matmul_residual_kernel(x_ref, w_ref, r_ref, o_ref):
    o_ref[...] = r_ref[...] + jnp.dot(x_ref[...], w_ref[...], preferred_element_type=F32)


def matmul_residual(x, w, res, *, tm=1024, tn=1024):
    m, k = x.shape
    n = w.shape[1]
    return pl.pallas_call(
        _matmul_residual_kernel,
        grid=(m // tm, n // tn),
        in_specs=[
            pl.BlockSpec((tm, k), lambda i, j: (i, 0)),
            pl.BlockSpec((k, tn), lambda i, j: (0, j)),
            pl.BlockSpec((tm, tn), lambda i, j: (i, j)),
        ],
        out_specs=pl.BlockSpec((tm, tn), lambda i, j: (i, j)),
        out_shape=jax.ShapeDtypeStruct((m, n), F32),
        compiler_params=_params("parallel", "arbitrary"),
    )(x, w, res)


def _scan_lanes(x, op, fill, is_suffix):
    s = x.shape[-1]
    lane = lax.broadcasted_iota(jnp.int32, x.shape, 1)
    d = 1
    while d < s:
        from_left = jnp.where(lane >= d, pltpu.roll(x, d, axis=1), fill)
        from_right = jnp.where(lane < s - d, pltpu.roll(x, s - d, axis=1), fill)
        x = op(x, jnp.where(is_suffix, from_right, from_left))
        d *= 2
    return x


def _gate_scan_kernel(ig_ref, fg_ref, bi_ref, bf_ref, a_ref, m_ref, e_ref):
    ig = ig_ref[0] + bi_ref[...]
    fg = fg_ref[0] + bf_ref[...]
    is_suffix = lax.broadcasted_iota(jnp.int32, ig.shape, 0) >= MLSTM_HEADS
    logsig = jnp.minimum(fg, 0.0) - jnp.log1p(jnp.exp(-jnp.abs(fg)))
    fcum = _scan_lanes(logsig, jnp.add, 0.0, is_suffix)
    a = ig - fcum
    mx = jnp.maximum(_scan_lanes(a, jnp.maximum, -jnp.inf, is_suffix), 0.0)
    a_ref[0] = a
    m_ref[0] = mx
    e_ref[0] = jnp.exp(-(fcum + mx))


def gate_scan(ig, fg, bi, bf):
    b, r, s = ig.shape
    g_spec = pl.BlockSpec((1, r, s), lambda i: (i, 0, 0))
    b_spec = pl.BlockSpec((r, 1), lambda i: (0, 0))
    o_shape = jax.ShapeDtypeStruct((b, r, s), F32)
    return pl.pallas_call(
        _gate_scan_kernel,
        grid=(b,),
        in_specs=[g_spec, g_spec, b_spec, b_spec],
        out_specs=[g_spec] * 3,
        out_shape=[o_shape] * 3,
        compiler_params=_params("parallel"),
    )(ig, fg, bi, bf)


def _mlstm_kernel(q_ref, k_ref, v_ref, o_ref, row_ref, col_ref, hn_ref, out_ref,
                  numf_ref, numb_ref, denf_ref, denb_ref, *, tq, nk):
    i = pl.program_id(2)
    q = q_ref[...]
    col = col_ref[0, 0]
    m_f, m_b, e_f, e_b = col[:, 0:1], col[:, 1:2], col[:, 2:3], col[:, 3:4]
    scale = MLSTM_QK_DIM ** -0.5

    numf_ref[...] = jnp.zeros_like(numf_ref)
    numb_ref[...] = jnp.zeros_like(numb_ref)
    denf_ref[...] = jnp.zeros_like(denf_ref)
    denb_ref[...] = jnp.zeros_like(denb_ref)

    def scores(j):
        start = pl.multiple_of(j * tq, tq)
        kj = k_ref[pl.ds(start, tq), :]
        vj = v_ref[pl.ds(start, tq), :]
        return lax.dot_general(q, kj, _NT, preferred_element_type=F32) * scale, vj

    def accumulate(s, vj, num_ref, den_ref):
        den_ref[...] += jnp.sum(s, axis=-1, keepdims=True)
        num_ref[...] += jnp.dot(s.astype(BF16), vj, preferred_element_type=F32)

    def fwd_block(j, carry):
        s, vj = scores(j)
        a = row_ref[0, 0, j][0:1, :]
        accumulate(s * jnp.exp(a - m_f), vj, numf_ref, denf_ref)
        return carry

    def bwd_block(j, carry):
        s, vj = scores(j)
        a = row_ref[0, 0, j][1:2, :]
        accumulate(s * jnp.exp(a - m_b), vj, numb_ref, denb_ref)
        return carry

    lax.fori_loop(0, i, fwd_block, 0)
    lax.fori_loop(i + 1, nk, bwd_block, 0)

    s, vj = scores(i)
    rows = row_ref[0, 0, i]
    t_idx = lax.broadcasted_iota(jnp.int32, (tq, tq), 0)
    s_idx = lax.broadcasted_iota(jnp.int32, (tq, tq), 1)
    d_f = jnp.where(s_idx <= t_idx, jnp.exp(rows[0:1, :] - m_f), 0.0)
    d_b = jnp.where(s_idx >= t_idx, jnp.exp(rows[1:2, :] - m_b), 0.0)
    accumulate(s * d_f, vj, numf_ref, denf_ref)
    accumulate(s * d_b, vj, numb_ref, denb_ref)

    h = (numf_ref[...] / jnp.maximum(jnp.abs(denf_ref[...]), e_f)
         + numb_ref[...] / jnp.maximum(jnp.abs(denb_ref[...]), e_b))
    h = _rms(h, hn_ref[0])
    out_ref[...] = (h * jax.nn.sigmoid(o_ref[...].astype(F32))).astype(out_ref.dtype)


def mlstm_core(proj, rowp, colp, head_norm, batch, seq, *, tq=256):
    h, dk, dv = MLSTM_HEADS, MLSTM_QK_DIM, MLSTM_V_DIM
    nq = seq // tq
    k_col0 = h * dk // dk
    v_col0 = 2 * h * dk // dv
    o_col0 = (2 * h * dk + h * dv) // dv
    kernel = functools.partial(_mlstm_kernel, tq=tq, nk=nq)
    return pl.pallas_call(
        kernel,
        grid=(batch, h, nq),
        in_specs=[
            pl.BlockSpec((tq, dk), lambda b, hh, i: (b * nq + i, hh)),
            pl.BlockSpec((seq, dk), lambda b, hh, i: (b, k_col0 + hh)),
            pl.BlockSpec((seq, dv), lambda b, hh, i: (b, v_col0 + hh)),
            pl.BlockSpec((tq, dv), lambda b, hh, i: (b * nq + i, o_col0 + hh)),
            pl.BlockSpec((1, 1, nq, 8, tq), lambda b, hh, i: (b, hh, 0, 0, 0)),
            pl.BlockSpec((1, 1, tq, 8), lambda b, hh, i: (b, hh, i, 0)),
            pl.BlockSpec((1, 1, dv), lambda b, hh, i: (hh, 0, 0)),
        ],
        out_specs=pl.BlockSpec((tq, dv), lambda b, hh, i: (b * nq + i, hh)),
        out_shape=jax.ShapeDtypeStruct((batch * seq, h * dv), BF16),
        scratch_shapes=[pltpu.VMEM((tq, dv), F32), pltpu.VMEM((tq, dv), F32),
                        pltpu.VMEM((tq, 1), F32), pltpu.VMEM((tq, 1), F32)],
        compiler_params=_params("parallel", "parallel", "arbitrary"),
    )(proj, proj, proj, proj, rowp, colp, head_norm.reshape(h, 1, dv))


def mlstm_mixer(x2d, ln, w_in, gate_bias, head_norm, w_out, batch, seq):
    h = MLSTM_HEADS
    n_main = 2 * h * MLSTM_QK_DIM + 2 * h * MLSTM_V_DIM
    w_main = w_in[:, :n_main].astype(BF16)
    w_gate = jnp.pad(w_in[:, n_main:], ((0, 0), (0, LANES - 4 * h))).astype(BF16)
    proj = norm_matmul(x2d, ln, w_main, BF16)
    gates = norm_matmul(x2d, ln, w_gate, F32)[:, :4 * h]
    gates_t = gates.reshape(batch, seq, 4, h).transpose(0, 2, 3, 1)
    bias = gate_bias.astype(F32)
    a, mx, en = gate_scan(gates_t[:, 0::2].reshape(batch, 2 * h, seq), gates_t[:, 1::2].reshape(batch, 2 * h, seq),
                          bias[0::2].reshape(2 * h, 1), bias[1::2].reshape(2 * h, 1))
    tq = MLSTM_BLOCK
    nk = seq // tq
    rowp = a.reshape(batch, 2, h, nk, tq).transpose(0, 2, 3, 1, 4)
    rowp = jnp.pad(rowp, ((0, 0), (0, 0), (0, 0), (0, 6), (0, 0)))
    colp = jnp.concatenate([mx.reshape(batch, 2, h, seq), en.reshape(batch, 2, h, seq)], axis=1)
    colp = jnp.pad(colp.transpose(0, 2, 3, 1), ((0, 0), (0, 0), (0, 0), (0, 4)))
    hg = mlstm_core(proj, rowp, colp, head_norm, batch, seq, tq=tq)
    return matmul_residual(hg, w_out.astype(BF16), x2d)


def _qk_proj_kernel(x_ref, g_ref, w_ref, gain_ref, mult_ref, cos_ref, sin_ref, o_ref, xn_ref):
    @pl.when(pl.program_id(1) == 0)
    def _():
        xn_ref[...] = _rms(x_ref[...], g_ref[...]).astype(BF16)

    acc = jnp.dot(xn_ref[...], w_ref[...], preferred_element_type=F32)
    hd = ATTN_HEAD_DIM
    cos = cos_ref[...]
    sin = sin_ref[...]
    for c in range(acc.shape[1] // hd):
        sl = slice(c * hd, (c + 1) * hd)
        y = _rms(acc[:, sl], gain_ref[:, sl])
        y = y * cos + pltpu.roll(y, hd // 2, axis=1) * sin
        o_ref[:, sl] = (y * mult_ref[:, sl]).astype(o_ref.dtype)


def qk_proj(x, g, w, gain, mult, cos_t, sin_t, seq, *, tm=512, tn=512):
    m, k = x.shape
    n = w.shape[1]
    nseq = seq // tm
    return pl.pallas_call(
        _qk_proj_kernel,
        grid=(m // tm, n // tn),
        in_specs=[
            pl.BlockSpec((tm, k), lambda i, j: (i, 0)),
            pl.BlockSpec((1, k), lambda i, j: (0, 0)),
            pl.BlockSpec((k, tn), lambda i, j: (0, j)),
            pl.BlockSpec((1, tn), lambda i, j: (0, j)),
            pl.BlockSpec((1, tn), lambda i, j: (0, j)),
            pl.BlockSpec((tm, ATTN_HEAD_DIM), lambda i, j: (i % nseq, 0)),
            pl.BlockSpec((tm, ATTN_HEAD_DIM), lambda i, j: (i % nseq, 0)),
        ],
        out_specs=pl.BlockSpec((tm, tn), lambda i, j: (i, j)),
        out_shape=jax.ShapeDtypeStruct((m, n), BF16),
        scratch_shapes=[pltpu.VMEM((tm, k), BF16)],
        compiler_params=_params("parallel", "arbitrary"),
    )(x, g.reshape(1, k), w, gain, mult, cos_t, sin_t)


def _attn_kernel(q_ref, k_ref, v_ref, o_ref):
    hd = ATTN_HEAD_DIM
    k = k_ref[...]
    v = v_ref[...]
    for g in range(ATTN_GROUPS):
        sl = slice(g * hd, (g + 1) * hd)
        s = lax.dot_general(q_ref[:, sl], k, _NT, preferred_element_type=F32)
        p = jnp.exp(s - jnp.max(s, axis=-1, keepdims=True))
        l = jnp.sum(p, axis=-1, keepdims=True)
        o = jnp.dot(p.astype(BF16), v, preferred_element_type=F32)
        o_ref[:, sl] = (o / l).astype(o_ref.dtype)


def attn_core(qk, v, batch, seq, *, tq=512):
    hd, g = ATTN_HEAD_DIM, ATTN_GROUPS
    nq = seq // tq
    return pl.pallas_call(
        _attn_kernel,
        grid=(batch, ATTN_KV_HEADS, nq),
        in_specs=[
            pl.BlockSpec((tq, g * hd), lambda b, kv, i: (b * nq + i, kv)),
            pl.BlockSpec((seq, hd), lambda b, kv, i: (b, ATTN_Q_HEADS + kv)),
            pl.BlockSpec((seq, hd), lambda b, kv, i: (b, kv)),
        ],
        out_specs=pl.BlockSpec((tq, g * hd), lambda b, kv, i: (b * nq + i, kv)),
        out_shape=jax.ShapeDtypeStruct((batch * seq, ATTN_Q_HEADS * hd), BF16),
        compiler_params=_params("parallel", "parallel", "arbitrary"),
    )(qk, qk, v)


def _rope_tables(seq):
    rows = seq // GRID_W
    row = jnp.repeat(jnp.arange(rows, dtype=F32), GRID_W)
    col = jnp.tile(jnp.arange(GRID_W, dtype=F32), rows)
    n_freq = ATTN_HEAD_DIM // 4
    freqs = ROPE_THETA ** (-jnp.arange(n_freq, dtype=F32) / n_freq)
    ang = jnp.concatenate([row[:, None] * freqs, col[:, None] * freqs], axis=-1)
    cos, sin = jnp.cos(ang), jnp.sin(ang)
    return jnp.concatenate([cos, cos], axis=-1), jnp.concatenate([-sin, sin], axis=-1)


def attn_mixer(x2d, ln, w_in, q_norm, k_norm, w_out, batch, seq):
    hd = ATTN_HEAD_DIM
    n_qk_heads = ATTN_Q_HEADS + ATTN_KV_HEADS
    n_qk = n_qk_heads * hd
    perm = jnp.concatenate([jnp.arange(0, hd, 2), jnp.arange(1, hd, 2)])
    d_model = w_in.shape[0]
    w_qk = w_in[:, :n_qk].reshape(d_model, n_qk_heads, hd)[:, :, perm].reshape(d_model, n_qk).astype(BF16)
    w_v = w_in[:, n_qk:].astype(BF16)
    gain = jnp.concatenate([jnp.tile(q_norm[perm], ATTN_Q_HEADS), jnp.tile(k_norm[perm], ATTN_KV_HEADS)])
    mult = jnp.concatenate([jnp.full((ATTN_Q_HEADS * hd,), hd ** -0.5, F32),
                            jnp.ones((ATTN_KV_HEADS * hd,), F32)])
    cos_t, sin_t = _rope_tables(seq)
    qk = qk_proj(x2d, ln, w_qk, gain.reshape(1, n_qk).astype(F32), mult.reshape(1, n_qk), cos_t, sin_t, seq)
    v = norm_matmul(x2d, ln, w_v, BF16)
    o = attn_core(qk, v, batch, seq)
    return matmul_residual(o, w_out.astype(BF16), x2d)


_PEER_CAND = [(i, j) for i in range(PEER_TOPK) for j in range(min(PEER_TOPK, (PEER_TOPK + 1) // (i + 1)))]
_PEER_CAND_ROWS = -(-len(_PEER_CAND) // 8) * 8


def _top_values(x, n):
    vals = []
    for _ in range(n):
        m = jnp.max(x, axis=0, keepdims=True)
        vals.append(m)
        x = jnp.where(x == m, -jnp.inf, x)
    return vals


def _peer_route_kernel(q_ref, k1_ref, k2_ref, thr_ref, c_ref, e2_ref, s2_ref, cand_ref):
    nk, half, topk = PEER_N_KEYS, PEER_HALF, PEER_TOPK
    k1 = k1_ref[...]
    k2 = k2_ref[...]
    tm = q_ref.shape[0]
    cand_ref[...] = jnp.full(cand_ref.shape, -jnp.inf, F32)
    for h in range(PEER_HEADS):
        qa = q_ref[:, (2 * h) * half:(2 * h + 1) * half]
        qb = q_ref[:, (2 * h + 1) * half:(2 * h + 2) * half]
        s1 = lax.dot_general(k1, qa, _NT, preferred_element_type=F32)
        s2 = lax.dot_general(k2, qb, _NT, preferred_element_type=F32)
        t1 = _top_values(s1, topk + 1)
        t2 = _top_values(s2, topk + 1)
        for r, (i, j) in enumerate(_PEER_CAND):
            cand_ref[r:r + 1, :] = t1[i] + t2[j]
        c = _top_values(cand_ref[...], topk + 1)
        c_next = jnp.maximum(c[topk], jnp.maximum(t1[topk] + t2[0], t1[0] + t2[topk]))
        thr = 0.5 * (c[topk - 1] + c_next)
        z = jnp.zeros((1, tm), F32)
        for kk in range(topk):
            z = z + jnp.exp(c[kk] - c[0])
        rows = slice(h * nk, (h + 1) * nk)
        e2 = jnp.exp(s2 - t2[0])
        for lc in range(tm // LANES):
            cols = slice(lc * LANES, (lc + 1) * LANES)
            e2_ref[lc, rows, :] = e2[:, cols]
            s2_ref[lc, rows, :] = s2[:, cols]
        n_sub = thr_ref.shape[1] // PEER_HEADS
        thr_a = thr - s1
        c_a = jnp.exp(s1 - t1[0]) / z
        for blk in range(nk // n_sub):
            thr_ref[blk, h * n_sub:(h + 1) * n_sub, :] = thr_a[blk * n_sub:(blk + 1) * n_sub, :]
            c_ref[blk, h * n_sub:(h + 1) * n_sub, :] = c_a[blk * n_sub:(blk + 1) * n_sub, :]


def peer_route(q, k1, k2, n_sub, *, tm=512):
    t, d = q.shape
    rows = PEER_HEADS * PEER_N_KEYS
    blocks = PEER_N_KEYS // n_sub
    a_spec = pl.BlockSpec((blocks, PEER_HEADS * n_sub, tm), lambda i: (0, 0, i))
    a_shape = jax.ShapeDtypeStruct((blocks, PEER_HEADS * n_sub, t), F32)
    b_spec = pl.BlockSpec((tm // LANES, rows, LANES), lambda i: (i, 0, 0))
    b_shape = jax.ShapeDtypeStruct((t // LANES, rows, LANES), F32)
    return pl.pallas_call(
        _peer_route_kernel,
        grid=(t // tm,),
        in_specs=[
            pl.BlockSpec((tm, d), lambda i: (i, 0)),
            pl.BlockSpec((PEER_N_KEYS, PEER_HALF), lambda i: (0, 0)),
            pl.BlockSpec((PEER_N_KEYS, PEER_HALF), lambda i: (0, 0)),
        ],
        out_specs=[a_spec, a_spec, b_spec, b_spec],
        out_shape=[a_shape, a_shape, b_shape, b_shape],
        scratch_shapes=[pltpu.VMEM((_PEER_CAND_ROWS, tm), F32)],
        compiler_params=_params("parallel"),
    )(q, k1, k2)


def _gelu_tanh(x):
    return 0.5 * x * (1.0 + jnp.tanh(math.sqrt(2.0 / math.pi) * (x + 0.044715 * (x * x * x))))


def _peer_expert_kernel(xt_ref, u_ref, vt_ref, thr_ref, c_ref, e2_ref, s2_ref, res_ref, zero_ref, o_ref,
                        acc_ref, w0_ref, w1_ref, g_ref, *, n_blocks):
    e = pl.program_id(1)
    nk = PEER_N_KEYS
    eb, d = u_ref.shape
    tt = xt_ref.shape[1]
    n_sub = eb // nk
    lane_tiles = tt // LANES
    n_stages = eb // PEER_ACT_CHUNK
    keys_per_stage = PEER_ACT_CHUNK // nk
    out_per_stage = d // PEER_OUT_CHUNK // n_stages

    def gate_tiles(a):
        zeros = []
        sub = PEER_GATE_ROWS
        for lc in range(lane_tiles):
            cols = slice(lc * LANES, (lc + 1) * LANES)
            folded = None
            for b0 in range(0, nk, sub):
                gate = jnp.zeros((sub, LANES), F32)
                for h in range(PEER_HEADS):
                    r = h * n_sub + a
                    rows = slice(h * nk + b0, h * nk + b0 + sub)
                    gate = gate + jnp.where(s2_ref[lc, rows, :] >= thr_ref[0, r:r + 1, cols],
                                            e2_ref[lc, rows, :] * c_ref[0, r:r + 1, cols], 0.0)
                g_ref[a * nk + b0:a * nk + b0 + sub, cols] = gate
                for r in range(0, sub, 8):
                    folded = gate[r:r + 8, :] if folded is None else folded + gate[r:r + 8, :]
            zeros.append(pltpu.bitcast(folded, jnp.uint32) & zero_ref[...])
        return zeros

    def paced(lhs, zeros):
        if not zeros:
            return lhs
        z = functools.reduce(jnp.bitwise_or, zeros)
        words = pltpu.bitcast(lhs, jnp.uint32)
        words = words | jnp.tile(z, (words.shape[0] // z.shape[0], words.shape[1] // z.shape[1]))
        return pltpu.bitcast(words, lhs.dtype)

    def run(w_old_ref, w_new_ref):
        for s in range(n_stages):
            zeros = []
            if w_new_ref is not None:
                for a in range(s * keys_per_stage, (s + 1) * keys_per_stage):
                    zeros += gate_tiles(a)
            if w_old_ref is not None:
                for k in range(s * out_per_stage, (s + 1) * out_per_stage):
                    rows = slice(k * PEER_OUT_CHUNK, (k + 1) * PEER_OUT_CHUNK)
                    share = len(zeros) // out_per_stage
                    lhs = paced(vt_ref[rows, :], zeros[(k % out_per_stage) * share:(k % out_per_stage + 1) * share])
                    acc_ref[rows, :] += jnp.dot(lhs, w_old_ref[...], preferred_element_type=F32)
            if w_new_ref is not None:
                rows = slice(s * PEER_ACT_CHUNK, (s + 1) * PEER_ACT_CHUNK)
                act = _gelu_tanh(jnp.dot(paced(u_ref[rows, :], zeros), xt_ref[...], preferred_element_type=F32))
                w_new_ref[rows, :] = (g_ref[rows, :] * act).astype(BF16)

    @pl.when(e == 0)
    def _():
        acc_ref[...] = jnp.zeros_like(acc_ref)
        run(None, w0_ref)

    is_mid = jnp.logical_and(e > 0, e < n_blocks)

    @pl.when(jnp.logical_and(is_mid, e % 2 == 1))
    def _():
        run(w0_ref, w1_ref)

    @pl.when(jnp.logical_and(is_mid, e % 2 == 0))
    def _():
        run(w1_ref, w0_ref)

    @pl.when(e == n_blocks)
    def _():
        run(w1_ref if n_blocks % 2 == 0 else w0_ref, None)
        o_ref[...] = res_ref[...] + acc_ref[...].T


def peer_experts(xt, u, vt, thr, c, e2, s2, res, *, tt=512):
    d, t = xt.shape
    n_exp = u.shape[0]
    eb = PEER_EXPERT_BLOCK
    n_blocks = n_exp // eb
    rows = PEER_HEADS * PEER_N_KEYS
    once = pl.Buffered(1)
    side_a = pl.BlockSpec((1, thr.shape[1], tt), lambda i, e: (jnp.minimum(e, n_blocks - 1), 0, i))
    side_b = pl.BlockSpec((tt // LANES, rows, LANES), lambda i, e: (i, 0, 0), pipeline_mode=once)
    kernel = functools.partial(_peer_expert_kernel, n_blocks=n_blocks)
    return pl.pallas_call(
        kernel,
        grid=(t // tt, n_blocks + 1),
        in_specs=[
            pl.BlockSpec((d, tt), lambda i, e: (0, i), pipeline_mode=once),
            pl.BlockSpec((eb, d), lambda i, e: (jnp.minimum(e, n_blocks - 1), 0)),
            pl.BlockSpec((d, eb), lambda i, e: (0, jnp.maximum(e - 1, 0))),
            side_a, side_a, side_b, side_b,
            pl.BlockSpec((tt, d), lambda i, e: (i, 0), pipeline_mode=once),
            pl.BlockSpec((8, LANES), lambda i, e: (0, 0)),
        ],
        out_specs=pl.BlockSpec((tt, d), lambda i, e: (i, 0)),
        out_shape=jax.ShapeDtypeStruct((t, d), F32),
        scratch_shapes=[pltpu.VMEM((d, tt), F32), pltpu.VMEM((eb, tt), BF16), pltpu.VMEM((eb, tt), BF16),
                        pltpu.VMEM((eb, tt), F32)],
        compiler_params=_params("parallel", "arbitrary"),
    )(xt, u, vt, thr, c, e2, s2, res, jnp.zeros((8, LANES), jnp.uint32))


def peer_mixer(x2d, ln, wq, k1, k2, u, v):
    q, xt = norm_matmul(x2d, ln, wq.astype(BF16), BF16, with_transposed_norm=True)
    thr, c, e2, s2 = peer_route(q, k1.astype(BF16), k2.astype(BF16), PEER_EXPERT_BLOCK // PEER_N_KEYS)
    return peer_experts(xt, u.astype(BF16), v.T.astype(BF16), thr, c, e2, s2, x2d)


def kernel(x, ln_mix, ln_ffn, mlstm_w_in, mlstm_gate_bias, mlstm_head_norm, mlstm_w_out,
           attn_w_in, attn_q_norm, attn_k_norm, attn_w_out,
           peer_wq, peer_k1, peer_k2, peer_u, peer_v):
    batch, seq, d_model = x.shape
    depth = ln_mix.shape[0]
    x2d = x.reshape(batch * seq, d_model)
    for layer in range(depth):
        j = layer // 2
        if layer % 2 == 0:
            x2d = mlstm_mixer(x2d, ln_mix[layer], mlstm_w_in[j], mlstm_gate_bias[j], mlstm_head_norm[j],
                              mlstm_w_out[j], batch, seq)
        else:
            x2d = attn_mixer(x2d, ln_mix[layer], attn_w_in[j], attn_q_norm[j], attn_k_norm[j],
                             attn_w_out[j], batch, seq)
        x2d = peer_mixer(x2d, ln_ffn[layer], peer_wq[layer], peer_k1[layer], peer_k2[layer],
                         peer_u[layer], peer_v[layer])
    return x2d.reshape(batch, seq, d_model)
```

```python
import functools
import math

import jax
import jax.numpy as jnp
from jax import lax
from jax.experimental import pallas as pl
from jax.experimental.pallas import tpu as pltpu

NORM_EPS = 1e-6

MLSTM_HEADS = 4
MLSTM_QK_DIM = 256
MLSTM_V_DIM = 512
MLSTM_BLOCK = 1024

ATTN_HEAD_DIM = 128
ATTN_Q_HEADS = 16
ATTN_KV_HEADS = 4
ATTN_GROUPS = ATTN_Q_HEADS // ATTN_KV_HEADS
GRID_W = 64
ROPE_THETA = 10000.0

PEER_HEADS = 8
PEER_N_KEYS = 128
PEER_TOPK = 16
PEER_HALF = 128
PEER_EXPERT_BLOCK = 1024
PEER_OUT_CHUNK = 256
PEER_ACT_CHUNK = 256
PEER_GATE_ROWS = 32

LANES = 128
VMEM_LIMIT_BYTES = 56 * 1024 * 1024

BF16 = jnp.bfloat16
F32 = jnp.float32

_NT = (((1,), (1,)), ((), ()))


def _params(*sem):
    return pltpu.CompilerParams(dimension_semantics=sem, vmem_limit_bytes=VMEM_LIMIT_BYTES)


def _rms(x, g):
    return x * lax.rsqrt(jnp.mean(x * x, axis=-1, keepdims=True) + NORM_EPS) * g


def _norm_matmul_kernel(x_ref, g_ref, w_ref, o_ref, xn_ref):
    @pl.when(pl.program_id(1) == 0)
    def _():
        xn_ref[...] = _rms(x_ref[...], g_ref[...]).astype(BF16)

    o_ref[...] = jnp.dot(xn_ref[...], w_ref[...], preferred_element_type=F32).astype(o_ref.dtype)


def _norm_matmul_t_kernel(x_ref, g_ref, w_ref, o_ref, xt_ref, xn_ref):
    @pl.when(pl.program_id(1) == 0)
    def _():
        y = _rms(x_ref[...], g_ref[...])
        xn_ref[...] = y.astype(BF16)
        xt_ref[...] = y.T.astype(BF16)

    o_ref[...] = jnp.dot(xn_ref[...], w_ref[...], preferred_element_type=F32).astype(o_ref.dtype)


def norm_matmul(x, g, w, out_dtype, *, tm=1024, tn=1024, with_transposed_norm=False):
    m, k = x.shape
    n = w.shape[1]
    tn = min(tn, n)
    grid = (m // tm, n // tn)
    in_specs = [
        pl.BlockSpec((tm, k), lambda i, j: (i, 0)),
        pl.BlockSpec((1, k), lambda i, j: (0, 0)),
        pl.BlockSpec((k, tn), lambda i, j: (0, j)),
    ]
    o_spec = pl.BlockSpec((tm, tn), lambda i, j: (i, j))
    scratch = [pltpu.VMEM((tm, k), BF16)]
    if with_transposed_norm:
        return pl.pallas_call(
            _norm_matmul_t_kernel,
            grid=grid, in_specs=in_specs,
            out_specs=[o_spec, pl.BlockSpec((k, tm), lambda i, j: (0, i))],
            out_shape=[jax.ShapeDtypeStruct((m, n), out_dtype), jax.ShapeDtypeStruct((k, m), BF16)],
            scratch_shapes=scratch,
            compiler_params=_params("parallel", "arbitrary"),
        )(x, g.reshape(1, k), w)
    return pl.pallas_call(
        _norm_matmul_kernel,
        grid=grid, in_specs=in_specs, out_specs=o_spec,
        out_shape=jax.ShapeDtypeStruct((m, n), out_dtype),
        scratch_shapes=scratch,
        compiler_params=_params("parallel", "arbitrary"),
    )(x, g.reshape(1, k), w)


def _matmul_residual_kernel(x_ref, w_ref, r_ref, o_ref):
    o_ref[...] = r_ref[...] + jnp.dot(x_ref[...], w_ref[...], preferred_element_type=F32)


def matmul_residual(x, w, res, *, tm=1024, tn=1024):
    m, k = x.shape
    n = w.shape[1]
    return pl.pallas_call(
        _matmul_residual_kernel,
        grid=(m // tm, n // tn),
        in_specs=[
            pl.BlockSpec((tm, k), lambda i, j: (i, 0)),
            pl.BlockSpec((k, tn), lambda i, j: (0, j)),
            pl.BlockSpec((tm, tn), lambda i, j: (i, j)),
        ],
        out_specs=pl.BlockSpec((tm, tn), lambda i, j: (i, j)),
        out_shape=jax.ShapeDtypeStruct((m, n), F32),
        compiler_params=_params("parallel", "arbitrary"),
    )(x, w, res)


def _scan_lanes(x, op, fill, is_suffix):
    s = x.shape[-1]
    lane = lax.broadcasted_iota(jnp.int32, x.shape, 1)
    d = 1
    while d < s:
        from_left = jnp.where(lane >= d, pltpu.roll(x, d, axis=1), fill)
        from_right = jnp.where(lane < s - d, pltpu.roll(x, s - d, axis=1), fill)
        x = op(x, jnp.where(is_suffix, from_right, from_left))
        d *= 2
    return x


def _gate_scan_kernel(ig_ref, fg_ref, bi_ref, bf_ref, a_ref, m_ref, e_ref):
    ig = ig_ref[0] + bi_ref[...]
    fg = fg_ref[0] + bf_ref[...]
    is_suffix = lax.broadcasted_iota(jnp.int32, ig.shape, 0) >= MLSTM_HEADS
    logsig = jnp.minimum(fg, 0.0) - jnp.log1p(jnp.exp(-jnp.abs(fg)))
    fcum = _scan_lanes(logsig, jnp.add, 0.0, is_suffix)
    a = ig - fcum
    mx = jnp.maximum(_scan_lanes(a, jnp.maximum, -jnp.inf, is_suffix), 0.0)
    a_ref[0] = a
    m_ref[0] = mx
    e_ref[0] = jnp.exp(-(fcum + mx))


def gate_scan(ig, fg, bi, bf):
    b, r, s = ig.shape
    g_spec = pl.BlockSpec((1, r, s), lambda i: (i, 0, 0))
    b_spec = pl.BlockSpec((r, 1), lambda i: (0, 0))
    o_shape = jax.ShapeDtypeStruct((b, r, s), F32)
    return pl.pallas_call(
        _gate_scan_kernel,
        grid=(b,),
        in_specs=[g_spec, g_spec, b_spec, b_spec],
        out_specs=[g_spec] * 3,
        out_shape=[o_shape] * 3,
        compiler_params=_params("parallel"),
    )(ig, fg, bi, bf)


def _mlstm_kernel(q_ref, k_ref, v_ref, o_ref, row_ref, col_ref, hn_ref, out_ref,
                  numf_ref, numb_ref, denf_ref, denb_ref, *, tq, nk):
    i = pl.program_id(2)
    q = q_ref[...]
    col = col_ref[0, 0]
    m_f, m_b, e_f, e_b = col[:, 0:1], col[:, 1:2], col[:, 2:3], col[:, 3:4]
    scale = MLSTM_QK_DIM ** -0.5

    numf_ref[...] = jnp.zeros_like(numf_ref)
    numb_ref[...] = jnp.zeros_like(numb_ref)
    denf_ref[...] = jnp.zeros_like(denf_ref)
    denb_ref[...] = jnp.zeros_like(denb_ref)

    def scores(j):
        start = pl.multiple_of(j * tq, tq)
        kj = k_ref[pl.ds(start, tq), :]
        vj = v_ref[pl.ds(start, tq), :]
        return lax.dot_general(q, kj, _NT, preferred_element_type=F32) * scale, vj

    def accumulate(s, vj, num_ref, den_ref):
        den_ref[...] += jnp.sum(s, axis=-1, keepdims=True)
        num_ref[...] += jnp.dot(s.astype(BF16), vj, preferred_element_type=F32)

    def fwd_block(j, carry):
        s, vj = scores(j)
        a = row_ref[0, 0, j][0:1, :]
        accumulate(s * jnp.exp(a - m_f), vj, numf_ref, denf_ref)
        return carry

    def bwd_block(j, carry):
        s, vj = scores(j)
        a = row_ref[0, 0, j][1:2, :]
        accumulate(s * jnp.exp(a - m_b), vj, numb_ref, denb_ref)
        return carry

    lax.fori_loop(0, i, fwd_block, 0)
    lax.fori_loop(i + 1, nk, bwd_block, 0)

    s, vj = scores(i)
    rows = row_ref[0, 0, i]
    t_idx = lax.broadcasted_iota(jnp.int32, (tq, tq), 0)
    s_idx = lax.broadcasted_iota(jnp.int32, (tq, tq), 1)
    d_f = jnp.where(s_idx <= t_idx, jnp.exp(rows[0:1, :] - m_f), 0.0)
    d_b = jnp.where(s_idx >= t_idx, jnp.exp(rows[1:2, :] - m_b), 0.0)
    accumulate(s * d_f, vj, numf_ref, denf_ref)
    accumulate(s * d_b, vj, numb_ref, denb_ref)

    h = (numf_ref[...] / jnp.maximum(jnp.abs(denf_ref[...]), e_f)
         + numb_ref[...] / jnp.maximum(jnp.abs(denb_ref[...]), e_b))
    h = _rms(h, hn_ref[0])
    out_ref[...] = (h * jax.nn.sigmoid(o_ref[...].astype(F32))).astype(out_ref.dtype)


def mlstm_core(proj, rowp, colp, head_norm, batch, seq, *, tq=256):
    h, dk, dv = MLSTM_HEADS, MLSTM_QK_DIM, MLSTM_V_DIM
    nq = seq // tq
    k_col0 = h * dk // dk
    v_col0 = 2 * h * dk // dv
    o_col0 = (2 * h * dk + h * dv) // dv
    kernel = functools.partial(_mlstm_kernel, tq=tq, nk=nq)
    return pl.pallas_call(
        kernel,
        grid=(batch, h, nq),
        in_specs=[
            pl.BlockSpec((tq, dk), lambda b, hh, i: (b * nq + i, hh)),
            pl.BlockSpec((seq, dk), lambda b, hh, i: (b, k_col0 + hh)),
            pl.BlockSpec((seq, dv), lambda b, hh, i: (b, v_col0 + hh)),
            pl.BlockSpec((tq, dv), lambda b, hh, i: (b * nq + i, o_col0 + hh)),
            pl.BlockSpec((1, 1, nq, 8, tq), lambda b, hh, i: (b, hh, 0, 0, 0)),
            pl.BlockSpec((1, 1, tq, 8), lambda b, hh, i: (b, hh, i, 0)),
            pl.BlockSpec((1, 1, dv), lambda b, hh, i: (hh, 0, 0)),
        ],
        out_specs=pl.BlockSpec((tq, dv), lambda b, hh, i: (b * nq + i, hh)),
        out_shape=jax.ShapeDtypeStruct((batch * seq, h * dv), BF16),
        scratch_shapes=[pltpu.VMEM((tq, dv), F32), pltpu.VMEM((tq, dv), F32),
                        pltpu.VMEM((tq, 1), F32), pltpu.VMEM((tq, 1), F32)],
        compiler_params=_params("parallel", "parallel", "arbitrary"),
    )(proj, proj, proj, proj, rowp, colp, head_norm.reshape(h, 1, dv))


def mlstm_mixer(x2d, ln, w_in, gate_bias, head_norm, w_out, batch, seq):
    h = MLSTM_HEADS
    n_main = 2 * h * MLSTM_QK_DIM + 2 * h * MLSTM_V_DIM
    w_main = w_in[:, :n_main].astype(BF16)
    w_gate = jnp.pad(w_in[:, n_main:], ((0, 0), (0, LANES - 4 * h))).astype(BF16)
    proj = norm_matmul(x2d, ln, w_main, BF16)
    gates = norm_matmul(x2d, ln, w_gate, F32)[:, :4 * h]
    gates_t = gates.reshape(batch, seq, 4, h).transpose(0, 2, 3, 1)
    bias = gate_bias.astype(F32)
    a, mx, en = gate_scan(gates_t[:, 0::2].reshape(batch, 2 * h, seq), gates_t[:, 1::2].reshape(batch, 2 * h, seq),
                          bias[0::2].reshape(2 * h, 1), bias[1::2].reshape(2 * h, 1))
    tq = MLSTM_BLOCK
    nk = seq // tq
    rowp = a.reshape(batch, 2, h, nk, tq).transpose(0, 2, 3, 1, 4)
    rowp = jnp.pad(rowp, ((0, 0), (0, 0), (0, 0), (0, 6), (0, 0)))
    colp = jnp.concatenate([mx.reshape(batch, 2, h, seq), en.reshape(batch, 2, h, seq)], axis=1)
    colp = jnp.pad(colp.transpose(0, 2, 3, 1), ((0, 0), (0, 0), (0, 0), (0, 4)))
    hg = mlstm_core(proj, rowp, colp, head_norm, batch, seq, tq=tq)
    return matmul_residual(hg, w_out.astype(BF16), x2d)


def _qk_proj_kernel(x_ref, g_ref, w_ref, gain_ref, mult_ref, cos_ref, sin_ref, o_ref, xn_ref):
    @pl.when(pl.program_id(1) == 0)
    def _():
        xn_ref[...] = _rms(x_ref[...], g_ref[...]).astype(BF16)

    acc = jnp.dot(xn_ref[...], w_ref[...], preferred_element_type=F32)
    hd = ATTN_HEAD_DIM
    cos = cos_ref[...]
    sin = sin_ref[...]
    for c in range(acc.shape[1] // hd):
        sl = slice(c * hd, (c + 1) * hd)
        y = _rms(acc[:, sl], gain_ref[:, sl])
        y = y * cos + pltpu.roll(y, hd // 2, axis=1) * sin
        o_ref[:, sl] = (y * mult_ref[:, sl]).astype(o_ref.dtype)


def qk_proj(x, g, w, gain, mult, cos_t, sin_t, seq, *, tm=512, tn=2560):
    m, k = x.shape
    n = w.shape[1]
    nseq = seq // tm
    return pl.pallas_call(
        _qk_proj_kernel,
        grid=(m // tm, n // tn),
        in_specs=[
            pl.BlockSpec((tm, k), lambda i, j: (i, 0)),
            pl.BlockSpec((1, k), lambda i, j: (0, 0)),
            pl.BlockSpec((k, tn), lambda i, j: (0, j)),
            pl.BlockSpec((1, tn), lambda i, j: (0, j)),
            pl.BlockSpec((1, tn), lambda i, j: (0, j)),
            pl.BlockSpec((tm, ATTN_HEAD_DIM), lambda i, j: (i % nseq, 0)),
            pl.BlockSpec((tm, ATTN_HEAD_DIM), lambda i, j: (i % nseq, 0)),
        ],
        out_specs=pl.BlockSpec((tm, tn), lambda i, j: (i, j)),
        out_shape=jax.ShapeDtypeStruct((m, n), BF16),
        scratch_shapes=[pltpu.VMEM((tm, k), BF16)],
        compiler_params=_params("parallel", "arbitrary"),
    )(x, g.reshape(1, k), w, gain, mult, cos_t, sin_t)


def _attn_kernel(q_ref, k_ref, v_ref, o_ref):
    hd = ATTN_HEAD_DIM
    k = k_ref[...]
    v = v_ref[...]
    for g in range(ATTN_GROUPS):
        sl = slice(g * hd, (g + 1) * hd)
        s = lax.dot_general(q_ref[:, sl], k, _NT, preferred_element_type=F32)
        p = jnp.exp(s - jnp.max(s, axis=-1, keepdims=True))
        l = jnp.sum(p, axis=-1, keepdims=True)
        o = jnp.dot(p.astype(BF16), v, preferred_element_type=F32)
        o_ref[:, sl] = (o / l).astype(o_ref.dtype)


def attn_core(qk, v, batch, seq, *, tq=512):
    hd, g = ATTN_HEAD_DIM, ATTN_GROUPS
    nq = seq // tq
    return pl.pallas_call(
        _attn_kernel,
        grid=(batch, ATTN_KV_HEADS, nq),
        in_specs=[
            pl.BlockSpec((tq, g * hd), lambda b, kv, i: (b * nq + i, kv)),
            pl.BlockSpec((seq, hd), lambda b, kv, i: (b, ATTN_Q_HEADS + kv)),
            pl.BlockSpec((seq, hd), lambda b, kv, i: (b, kv)),
        ],
        out_specs=pl.BlockSpec((tq, g * hd), lambda b, kv, i: (b * nq + i, kv)),
        out_shape=jax.ShapeDtypeStruct((batch * seq, ATTN_Q_HEADS * hd), BF16),
        compiler_params=_params("parallel", "parallel", "arbitrary"),
    )(qk, qk, v)


def _rope_tables(seq):
    rows = seq // GRID_W
    row = jnp.repeat(jnp.arange(rows, dtype=F32), GRID_W)
    col = jnp.tile(jnp.arange(GRID_W, dtype=F32), rows)
    n_freq = ATTN_HEAD_DIM // 4
    freqs = ROPE_THETA ** (-jnp.arange(n_freq, dtype=F32) / n_freq)
    ang = jnp.concatenate([row[:, None] * freqs, col[:, None] * freqs], axis=-1)
    cos, sin = jnp.cos(ang), jnp.sin(ang)
    return jnp.concatenate([cos, cos], axis=-1), jnp.concatenate([-sin, sin], axis=-1)


def attn_mixer(x2d, ln, w_in, q_norm, k_norm, w_out, batch, seq):
    hd = ATTN_HEAD_DIM
    n_qk_heads = ATTN_Q_HEADS + ATTN_KV_HEADS
    n_qk = n_qk_heads * hd
    perm = jnp.concatenate([jnp.arange(0, hd, 2), jnp.arange(1, hd, 2)])
    d_model = w_in.shape[0]
    w_qk = w_in[:, :n_qk].reshape(d_model, n_qk_heads, hd)[:, :, perm].reshape(d_model, n_qk).astype(BF16)
    w_v = w_in[:, n_qk:].astype(BF16)
    gain = jnp.concatenate([jnp.tile(q_norm[perm], ATTN_Q_HEADS), jnp.tile(k_norm[perm], ATTN_KV_HEADS)])
    mult = jnp.concatenate([jnp.full((ATTN_Q_HEADS * hd,), hd ** -0.5, F32),
                            jnp.ones((ATTN_KV_HEADS * hd,), F32)])
    cos_t, sin_t = _rope_tables(seq)
    qk = qk_proj(x2d, ln, w_qk, gain.reshape(1, n_qk).astype(F32), mult.reshape(1, n_qk), cos_t, sin_t, seq)
    v = norm_matmul(x2d, ln, w_v, BF16)
    o = attn_core(qk, v, batch, seq)
    return matmul_residual(o, w_out.astype(BF16), x2d)


_PEER_CAND = [(i, j) for i in range(PEER_TOPK) for j in range(min(PEER_TOPK, (PEER_TOPK + 1) // (i + 1)))]
_PEER_CAND_ROWS = -(-len(_PEER_CAND) // 8) * 8


def _top_values(x, n):
    vals = []
    for _ in range(n):
        m = jnp.max(x, axis=0, keepdims=True)
        vals.append(m)
        x = jnp.where(x == m, -jnp.inf, x)
    return vals


def _peer_route_kernel(q_ref, k1_ref, k2_ref, thr_ref, c_ref, e2_ref, s2_ref, cand_ref):
    nk, half, topk = PEER_N_KEYS, PEER_HALF, PEER_TOPK
    k1 = k1_ref[...]
    k2 = k2_ref[...]
    tm = q_ref.shape[0]
    cand_ref[...] = jnp.full(cand_ref.shape, -jnp.inf, F32)
    for h in range(PEER_HEADS):
        qa = q_ref[:, (2 * h) * half:(2 * h + 1) * half]
        qb = q_ref[:, (2 * h + 1) * half:(2 * h + 2) * half]
        s1 = lax.dot_general(k1, qa, _NT, preferred_element_type=F32)
        s2 = lax.dot_general(k2, qb, _NT, preferred_element_type=F32)
        t1 = _top_values(s1, topk + 1)
        t2 = _top_values(s2, topk + 1)
        for r, (i, j) in enumerate(_PEER_CAND):
            cand_ref[r:r + 1, :] = t1[i] + t2[j]
        c = _top_values(cand_ref[...], topk + 1)
        c_next = jnp.maximum(c[topk], jnp.maximum(t1[topk] + t2[0], t1[0] + t2[topk]))
        thr = 0.5 * (c[topk - 1] + c_next)
        z = jnp.zeros((1, tm), F32)
        for kk in range(topk):
            z = z + jnp.exp(c[kk] - c[0])
        rows = slice(h * nk, (h + 1) * nk)
        e2 = jnp.exp(s2 - t2[0])
        for lc in range(tm // LANES):
            cols = slice(lc * LANES, (lc + 1) * LANES)
            e2_ref[lc, rows, :] = e2[:, cols]
            s2_ref[lc, rows, :] = s2[:, cols]
        n_sub = thr_ref.shape[1] // PEER_HEADS
        thr_a = thr - s1
        c_a = jnp.exp(s1 - t1[0]) / z
        for blk in range(nk // n_sub):
            thr_ref[blk, h * n_sub:(h + 1) * n_sub, :] = thr_a[blk * n_sub:(blk + 1) * n_sub, :]
            c_ref[blk, h * n_sub:(h + 1) * n_sub, :] = c_a[blk * n_sub:(blk + 1) * n_sub, :]


def peer_route(q, k1, k2, n_sub, *, tm=512):
    t, d = q.shape
    rows = PEER_HEADS * PEER_N_KEYS
    blocks = PEER_N_KEYS // n_sub
    a_spec = pl.BlockSpec((blocks, PEER_HEADS * n_sub, tm), lambda i: (0, 0, i))
    a_shape = jax.ShapeDtypeStruct((blocks, PEER_HEADS * n_sub, t), F32)
    b_spec = pl.BlockSpec((tm // LANES, rows, LANES), lambda i: (i, 0, 0))
    b_shape = jax.ShapeDtypeStruct((t // LANES, rows, LANES), F32)
    return pl.pallas_call(
        _peer_route_kernel,
        grid=(t // tm,),
        in_specs=[
            pl.BlockSpec((tm, d), lambda i: (i, 0)),
            pl.BlockSpec((PEER_N_KEYS, PEER_HALF), lambda i: (0, 0)),
            pl.BlockSpec((PEER_N_KEYS, PEER_HALF), lambda i: (0, 0)),
        ],
        out_specs=[a_spec, a_spec, b_spec, b_spec],
        out_shape=[a_shape, a_shape, b_shape, b_shape],
        scratch_shapes=[pltpu.VMEM((_PEER_CAND_ROWS, tm), F32)],
        compiler_params=_params("parallel"),
    )(q, k1, k2)


def _gelu_tanh(x):
    return 0.5 * x * (1.0 + jnp.tanh(math.sqrt(2.0 / math.pi) * (x + 0.044715 * (x * x * x))))


def _peer_expert_kernel(xt_ref, u_ref, vt_ref, thr_ref, c_ref, e2_ref, s2_ref, res_ref, zero_ref, o_ref,
                        acc_ref, w0_ref, w1_ref, g_ref, *, n_blocks):
    e = pl.program_id(1)
    nk = PEER_N_KEYS
    eb, d = u_ref.shape
    tt = xt_ref.shape[1]
    n_sub = eb // nk
    lane_tiles = tt // LANES
    n_stages = eb // PEER_ACT_CHUNK
    keys_per_stage = PEER_ACT_CHUNK // nk
    out_per_stage = d // PEER_OUT_CHUNK // n_stages

    def gate_tiles(a):
        zeros = []
        sub = PEER_GATE_ROWS
        for lc in range(lane_tiles):
            cols = slice(lc * LANES, (lc + 1) * LANES)
            folded = None
            for b0 in range(0, nk, sub):
                gate = jnp.zeros((sub, LANES), F32)
                for h in range(PEER_HEADS):
                    r = h * n_sub + a
                    rows = slice(h * nk + b0, h * nk + b0 + sub)
                    gate = gate + jnp.where(s2_ref[lc, rows, :] >= thr_ref[0, r:r + 1, cols],
                                            e2_ref[lc, rows, :] * c_ref[0, r:r + 1, cols], 0.0)
                g_ref[a * nk + b0:a * nk + b0 + sub, cols] = gate
                for r in range(0, sub, 8):
                    folded = gate[r:r + 8, :] if folded is None else folded + gate[r:r + 8, :]
            zeros.append(pltpu.bitcast(folded, jnp.uint32) & zero_ref[...])
        return zeros

    def paced(lhs, zeros):
        if not zeros:
            return lhs
        z = functools.reduce(jnp.bitwise_or, zeros)
        words = pltpu.bitcast(lhs, jnp.uint32)
        words = words | jnp.tile(z, (words.shape[0] // z.shape[0], words.shape[1] // z.shape[1]))
        return pltpu.bitcast(words, lhs.dtype)

    def run(w_old_ref, w_new_ref):
        for s in range(n_stages):
            zeros = []
            if w_new_ref is not None:
                for a in range(s * keys_per_stage, (s + 1) * keys_per_stage):
                    zeros += gate_tiles(a)
            if w_old_ref is not None:
                for k in range(s * out_per_stage, (s + 1) * out_per_stage):
                    rows = slice(k * PEER_OUT_CHUNK, (k + 1) * PEER_OUT_CHUNK)
                    share = len(zeros) // out_per_stage
                    lhs = paced(vt_ref[rows, :], zeros[(k % out_per_stage) * share:(k % out_per_stage + 1) * share])
                    acc_ref[rows, :] += jnp.dot(lhs, w_old_ref[...], preferred_element_type=F32)
            if w_new_ref is not None:
                rows = slice(s * PEER_ACT_CHUNK, (s + 1) * PEER_ACT_CHUNK)
                act = _gelu_tanh(jnp.dot(paced(u_ref[rows, :], zeros), xt_ref[...], preferred_element_type=F32))
                w_new_ref[rows, :] = (g_ref[rows, :] * act).astype(BF16)

    @pl.when(e == 0)
    def _():
        acc_ref[...] = jnp.zeros_like(acc_ref)
        run(None, w0_ref)

    is_mid = jnp.logical_and(e > 0, e < n_blocks)

    @pl.when(jnp.logical_and(is_mid, e % 2 == 1))
    def _():
        run(w0_ref, w1_ref)

    @pl.when(jnp.logical_and(is_mid, e % 2 == 0))
    def _():
        run(w1_ref, w0_ref)

    @pl.when(e == n_blocks)
    def _():
        run(w1_ref if n_blocks % 2 == 0 else w0_ref, None)
        o_ref[...] = res_ref[...] + acc_ref[...].T


def peer_experts(xt, u, vt, thr, c, e2, s2, res, *, tt=512):
    d, t = xt.shape
    n_exp = u.shape[0]
    eb = PEER_EXPERT_BLOCK
    n_blocks = n_exp // eb
    rows = PEER_HEADS * PEER_N_KEYS
    once = pl.Buffered(1)
    side_a = pl.BlockSpec((1, thr.shape[1], tt), lambda i, e: (jnp.minimum(e, n_blocks - 1), 0, i))
    side_b = pl.BlockSpec((tt // LANES, rows, LANES), lambda i, e: (i, 0, 0), pipeline_mode=once)
    kernel = functools.partial(_peer_expert_kernel, n_blocks=n_blocks)
    return pl.pallas_call(
        kernel,
        grid=(t // tt, n_blocks + 1),
        in_specs=[
            pl.BlockSpec((d, tt), lambda i, e: (0, i), pipeline_mode=once),
            pl.BlockSpec((eb, d), lambda i, e: (jnp.minimum(e, n_blocks - 1), 0)),
            pl.BlockSpec((d, eb), lambda i, e: (0, jnp.maximum(e - 1, 0))),
            side_a, side_a, side_b, side_b,
            pl.BlockSpec((tt, d), lambda i, e: (i, 0), pipeline_mode=once),
            pl.BlockSpec((8, LANES), lambda i, e: (0, 0)),
        ],
        out_specs=pl.BlockSpec((tt, d), lambda i, e: (i, 0)),
        out_shape=jax.ShapeDtypeStruct((t, d), F32),
        scratch_shapes=[pltpu.VMEM((d, tt), F32), pltpu.VMEM((eb, tt), BF16), pltpu.VMEM((eb, tt), BF16),
                        pltpu.VMEM((eb, tt), F32)],
        compiler_params=_params("parallel", "arbitrary"),
    )(xt, u, vt, thr, c, e2, s2, res, jnp.zeros((8, LANES), jnp.uint32))


def peer_mixer(x2d, ln, wq, k1, k2, u, v):
    q, xt = norm_matmul(x2d, ln, wq.astype(BF16), BF16, with_transposed_norm=True)
    thr, c, e2, s2 = peer_route(q, k1.astype(BF16), k2.astype(BF16), PEER_EXPERT_BLOCK // PEER_N_KEYS)
    return peer_experts(xt, u.astype(BF16), v.T.astype(BF16), thr, c, e2, s2, x2d)


def kernel(x, ln_mix, ln_ffn, mlstm_w_in, mlstm_gate_bias, mlstm_head_norm, mlstm_w_out,
           attn_w_in, attn_q_norm, attn_k_norm, attn_w_out,
           peer_wq, peer_k1, peer_k2, peer_u, peer_v):
    batch, seq, d_model = x.shape
    depth = ln_mix.shape[0]
    x2d = x.reshape(batch * seq, d_model)
    for layer in range(depth):
        j = layer // 2
        if layer % 2 == 0:
            x2d = mlstm_mixer(x2d, ln_mix[layer], mlstm_w_in[j], mlstm_gate_bias[j], mlstm_head_norm[j],
                              mlstm_w_out[j], batch, seq)
        else:
            x2d = attn_mixer(x2d, ln_mix[layer], attn_w_in[j], attn_q_norm[j], attn_k_norm[j],
                             attn_w_out[j], batch, seq)
        x2d = peer_mixer(x2d, ln_ffn[layer], peer_wq[layer], peer_k1[layer], peer_k2[layer],
                         peer_u[layer], peer_v[layer])
    return x2d.reshape(batch, seq, d_model)
```

```python
import functools
import math

import jax
import jax.numpy as jnp
from jax import lax
from jax.experimental import pallas as pl
from jax.experimental.pallas import tpu as pltpu

NORM_EPS = 1e-6

MLSTM_HEADS = 4
MLSTM_QK_DIM = 256
MLSTM_V_DIM = 512
MLSTM_BLOCK = 1024

ATTN_HEAD_DIM = 128
ATTN_Q_HEADS = 16
ATTN_KV_HEADS = 4
ATTN_GROUPS = ATTN_Q_HEADS // ATTN_KV_HEADS
GRID_W = 64
ROPE_THETA = 10000.0

PEER_HEADS = 8
PEER_N_KEYS = 128
PEER_TOPK = 16
PEER_HALF = 128
PEER_EXPERT_BLOCK = 1024
PEER_OUT_CHUNK = 256
PEER_ACT_CHUNK = 256
PEER_GATE_ROWS = 32

LANES = 128
VMEM_LIMIT_BYTES = 56 * 1024 * 1024

BF16 = jnp.bfloat16
F32 = jnp.float32

_NT = (((1,), (1,)), ((), ()))


def _params(*sem):
    return pltpu.CompilerParams(dimension_semantics=sem, vmem_limit_bytes=VMEM_LIMIT_BYTES)


def _rms(x, g):
    return x * lax.rsqrt(jnp.mean(x * x, axis=-1, keepdims=True) + NORM_EPS) * g


def _norm_matmul_kernel(x_ref, g_ref, w_ref, o_ref, xn_ref):
    @pl.when(pl.program_id(1) == 0)
    def _():
        xn_ref[...] = _rms(x_ref[...], g_ref[...]).astype(BF16)

    o_ref[...] = jnp.dot(xn_ref[...], w_ref[...], preferred_element_type=F32).astype(o_ref.dtype)


def _norm_matmul_t_kernel(x_ref, g_ref, w_ref, o_ref, xt_ref, xn_ref):
    @pl.when(pl.program_id(1) == 0)
    def _():
        y = _rms(x_ref[...], g_ref[...])
        xn_ref[...] = y.astype(BF16)
        xt_ref[...] = y.T.astype(BF16)

    o_ref[...] = jnp.dot(xn_ref[...], w_ref[...], preferred_element_type=F32).astype(o_ref.dtype)


def norm_matmul(x, g, w, out_dtype, *, tm=1024, tn=1024, with_transposed_norm=False):
    m, k = x.shape
    n = w.shape[1]
    tn = min(tn, n)
    grid = (m // tm, n // tn)
    in_specs = [
        pl.BlockSpec((tm, k), lambda i, j: (i, 0)),
        pl.BlockSpec((1, k), lambda i, j: (0, 0)),
        pl.BlockSpec((k, tn), lambda i, j: (0, j)),
    ]
    o_spec = pl.BlockSpec((tm, tn), lambda i, j: (i, j))
    scratch = [pltpu.VMEM((tm, k), BF16)]
    if with_transposed_norm:
        return pl.pallas_call(
            _norm_matmul_t_kernel,
            grid=grid, in_specs=in_specs,
            out_specs=[o_spec, pl.BlockSpec((k, tm), lambda i, j: (0, i))],
            out_shape=[jax.ShapeDtypeStruct((m, n), out_dtype), jax.ShapeDtypeStruct((k, m), BF16)],
            scratch_shapes=scratch,
            compiler_params=_params("parallel", "arbitrary"),
        )(x, g.reshape(1, k), w)
    return pl.pallas_call(
        _norm_matmul_kernel,
        grid=grid, in_specs=in_specs, out_specs=o_spec,
        out_shape=jax.ShapeDtypeStruct((m, n), out_dtype),
        scratch_shapes=scratch,
        compiler_params=_params("parallel", "arbitrary"),
    )(x, g.reshape(1, k), w)


def _matmul_residual_kernel(x_ref, w_ref, r_ref, o_ref):
    o_ref[...] = r_ref[...] + jnp.dot(x_ref[...], w_ref[...], preferred_element_type=F32)


def matmul_residual(x, w, res, *, tm=1024, tn=1024):
    m, k = x.shape
    n = w.shape[1]
    return pl.pallas_call(
        _matmul_residual_kernel,
        grid=(m // tm, n // tn),
        in_specs=[
            pl.BlockSpec((tm, k), lambda i, j: (i, 0)),
            pl.BlockSpec((k, tn), lambda i, j: (0, j)),
            pl.BlockSpec((tm, tn), lambda i, j: (i, j)),
        ],
        out_specs=pl.BlockSpec((tm, tn), lambda i, j: (i, j)),
        out_shape=jax.ShapeDtypeStruct((m, n), F32),
        compiler_params=_params("parallel", "arbitrary"),
    )(x, w, res)


def _scan_lanes(x, op, fill, is_suffix):
    s = x.shape[-1]
    lane = lax.broadcasted_iota(jnp.int32, x.shape, 1)
    d = 1
    while d < s:
        from_left = jnp.where(lane >= d, pltpu.roll(x, d, axis=1), fill)
        from_right = jnp.where(lane < s - d, pltpu.roll(x, s - d, axis=1), fill)
        x = op(x, jnp.where(is_suffix, from_right, from_left))
        d *= 2
    return x


def _gate_scan_kernel(ig_ref, fg_ref, bi_ref, bf_ref, a_ref, m_ref, e_ref):
    ig = ig_ref[0] + bi_ref[...]
    fg = fg_ref[0] + bf_ref[...]
    is_suffix = lax.broadcasted_iota(jnp.int32, ig.shape, 0) >= MLSTM_HEADS
    logsig = jnp.minimum(fg, 0.0) - jnp.log1p(jnp.exp(-jnp.abs(fg)))
    fcum = _scan_lanes(logsig, jnp.add, 0.0, is_suffix)
    a = ig - fcum
    mx = jnp.maximum(_scan_lanes(a, jnp.maximum, -jnp.inf, is_suffix), 0.0)
    a_ref[0] = a
    m_ref[0] = mx
    e_ref[0] = jnp.exp(-(fcum + mx))


def gate_scan(ig, fg, bi, bf):
    b, r, s = ig.shape
    g_spec = pl.BlockSpec((1, r, s), lambda i: (i, 0, 0))
    b_spec = pl.BlockSpec((r, 1), lambda i: (0, 0))
    o_shape = jax.ShapeDtypeStruct((b, r, s), F32)
    return pl.pallas_call(
        _gate_scan_kernel,
        grid=(b,),
        in_specs=[g_spec, g_spec, b_spec, b_spec],
        out_specs=[g_spec] * 3,
        out_shape=[o_shape] * 3,
        compiler_params=_params("parallel"),
    )(ig, fg, bi, bf)


def _mlstm_kernel(q_ref, k_ref, v_ref, o_ref, row_ref, col_ref, hn_ref, out_ref,
                  numf_ref, numb_ref, denf_ref, denb_ref, *, tq, nk):
    i = pl.program_id(2)
    q = q_ref[...]
    col = col_ref[0, 0]
    m_f, m_b, e_f, e_b = col[:, 0:1], col[:, 1:2], col[:, 2:3], col[:, 3:4]
    scale = MLSTM_QK_DIM ** -0.5

    numf_ref[...] = jnp.zeros_like(numf_ref)
    numb_ref[...] = jnp.zeros_like(numb_ref)
    denf_ref[...] = jnp.zeros_like(denf_ref)
    denb_ref[...] = jnp.zeros_like(denb_ref)

    def scores(j):
        start = pl.multiple_of(j * tq, tq)
        kj = k_ref[pl.ds(start, tq), :]
        vj = v_ref[pl.ds(start, tq), :]
        return lax.dot_general(q, kj, _NT, preferred_element_type=F32) * scale, vj

    def accumulate(s, vj, num_ref, den_ref):
        den_ref[...] += jnp.sum(s, axis=-1, keepdims=True)
        num_ref[...] += jnp.dot(s.astype(BF16), vj, preferred_element_type=F32)

    def fwd_block(j, carry):
        s, vj = scores(j)
        a = row_ref[0, 0, j][0:1, :]
        accumulate(s * jnp.exp(a - m_f), vj, numf_ref, denf_ref)
        return carry

    def bwd_block(j, carry):
        s, vj = scores(j)
        a = row_ref[0, 0, j][1:2, :]
        accumulate(s * jnp.exp(a - m_b), vj, numb_ref, denb_ref)
        return carry

    lax.fori_loop(0, i, fwd_block, 0)
    lax.fori_loop(i + 1, nk, bwd_block, 0)

    s, vj = scores(i)
    rows = row_ref[0, 0, i]
    t_idx = lax.broadcasted_iota(jnp.int32, (tq, tq), 0)
    s_idx = lax.broadcasted_iota(jnp.int32, (tq, tq), 1)
    d_f = jnp.where(s_idx <= t_idx, jnp.exp(rows[0:1, :] - m_f), 0.0)
    d_b = jnp.where(s_idx >= t_idx, jnp.exp(rows[1:2, :] - m_b), 0.0)
    accumulate(s * d_f, vj, numf_ref, denf_ref)
    accumulate(s * d_b, vj, numb_ref, denb_ref)

    h = (numf_ref[...] / jnp.maximum(jnp.abs(denf_ref[...]), e_f)
         + numb_ref[...] / jnp.maximum(jnp.abs(denb_ref[...]), e_b))
    h = _rms(h, hn_ref[0])
    out_ref[...] = (h * jax.nn.sigmoid(o_ref[...].astype(F32))).astype(out_ref.dtype)


def mlstm_core(proj, rowp, colp, head_norm, batch, seq, *, tq=256):
    h, dk, dv = MLSTM_HEADS, MLSTM_QK_DIM, MLSTM_V_DIM
    nq = seq // tq
    k_col0 = h * dk // dk
    v_col0 = 2 * h * dk // dv
    o_col0 = (2 * h * dk + h * dv) // dv
    kernel = functools.partial(_mlstm_kernel, tq=tq, nk=nq)
    return pl.pallas_call(
        kernel,
        grid=(batch, h, nq),
        in_specs=[
            pl.BlockSpec((tq, dk), lambda b, hh, i: (b * nq + i, hh)),
            pl.BlockSpec((seq, dk), lambda b, hh, i: (b, k_col0 + hh)),
            pl.BlockSpec((seq, dv), lambda b, hh, i: (b, v_col0 + hh)),
            pl.BlockSpec((tq, dv), lambda b, hh, i: (b * nq + i, o_col0 + hh)),
            pl.BlockSpec((1, 1, nq, 8, tq), lambda b, hh, i: (b, hh, 0, 0, 0)),
            pl.BlockSpec((1, 1, tq, 8), lambda b, hh, i: (b, hh, i, 0)),
            pl.BlockSpec((1, 1, dv), lambda b, hh, i: (hh, 0, 0)),
        ],
        out_specs=pl.BlockSpec((tq, dv), lambda b, hh, i: (b * nq + i, hh)),
        out_shape=jax.ShapeDtypeStruct((batch * seq, h * dv), BF16),
        scratch_shapes=[pltpu.VMEM((tq, dv), F32), pltpu.VMEM((tq, dv), F32),
                        pltpu.VMEM((tq, 1), F32), pltpu.VMEM((tq, 1), F32)],
        compiler_params=_params("parallel", "parallel", "arbitrary"),
    )(proj, proj, proj, proj, rowp, colp, head_norm.reshape(h, 1, dv))


def mlstm_mixer(x2d, ln, w_in, gate_bias, head_norm, w_out, batch, seq):
    h = MLSTM_HEADS
    n_main = 2 * h * MLSTM_QK_DIM + 2 * h * MLSTM_V_DIM
    w_main = w_in[:, :n_main].astype(BF16)
    w_gate = jnp.pad(w_in[:, n_main:], ((0, 0), (0, LANES - 4 * h))).astype(BF16)
    proj = norm_matmul(x2d, ln, w_main, BF16)
    gates = norm_matmul(x2d, ln, w_gate, F32)[:, :4 * h]
    gates_t = gates.reshape(batch, seq, 4, h).transpose(0, 2, 3, 1)
    bias = gate_bias.astype(F32)
    a, mx, en = gate_scan(gates_t[:, 0::2].reshape(batch, 2 * h, seq), gates_t[:, 1::2].reshape(batch, 2 * h, seq),
                          bias[0::2].reshape(2 * h, 1), bias[1::2].reshape(2 * h, 1))
    tq = MLSTM_BLOCK
    nk = seq // tq
    rowp = a.reshape(batch, 2, h, nk, tq).transpose(0, 2, 3, 1, 4)
    rowp = jnp.pad(rowp, ((0, 0), (0, 0), (0, 0), (0, 6), (0, 0)))
    colp = jnp.concatenate([mx.reshape(batch, 2, h, seq), en.reshape(batch, 2, h, seq)], axis=1)
    colp = jnp.pad(colp.transpose(0, 2, 3, 1), ((0, 0), (0, 0), (0, 0), (0, 4)))
    hg = mlstm_core(proj, rowp, colp, head_norm, batch, seq, tq=tq)
    return matmul_residual(hg, w_out.astype(BF16), x2d)


def _qk_proj_kernel(x_ref, g_ref, w_ref, gain_ref, mult_ref, cos_ref, sin_ref, o_ref, xn_ref):
    @pl.when(pl.program_id(1) == 0)
    def _():
        xn_ref[...] = _rms(x_ref[...], g_ref[...]).astype(BF16)

    acc = jnp.dot(xn_ref[...], w_ref[...], preferred_element_type=F32)
    hd = ATTN_HEAD_DIM
    cos = cos_ref[...]
    sin = sin_ref[...]
    for c in range(acc.shape[1] // hd):
        sl = slice(c * hd, (c + 1) * hd)
        y = _rms(acc[:, sl], gain_ref[:, sl])
        y = y * cos + pltpu.roll(y, hd // 2, axis=1) * sin
        o_ref[:, sl] = (y * mult_ref[:, sl]).astype(o_ref.dtype)


def qk_proj(x, g, w, gain, mult, cos_t, sin_t, seq, *, tm=512, tn=2560):
    m, k = x.shape
    n = w.shape[1]
    nseq = seq // tm
    return pl.pallas_call(
        _qk_proj_kernel,
        grid=(m // tm, n // tn),
        in_specs=[
            pl.BlockSpec((tm, k), lambda i, j: (i, 0)),
            pl.BlockSpec((1, k), lambda i, j: (0, 0)),
            pl.BlockSpec((k, tn), lambda i, j: (0, j)),
            pl.BlockSpec((1, tn), lambda i, j: (0, j)),
            pl.BlockSpec((1, tn), lambda i, j: (0, j)),
            pl.BlockSpec((tm, ATTN_HEAD_DIM), lambda i, j: (i % nseq, 0)),
            pl.BlockSpec((tm, ATTN_HEAD_DIM), lambda i, j: (i % nseq, 0)),
        ],
        out_specs=pl.BlockSpec((tm, tn), lambda i, j: (i, j)),
        out_shape=jax.ShapeDtypeStruct((m, n), BF16),
        scratch_shapes=[pltpu.VMEM((tm, k), BF16)],
        compiler_params=_params("parallel", "arbitrary"),
    )(x, g.reshape(1, k), w, gain, mult, cos_t, sin_t)


def _attn_kernel(q_ref, k_ref, v_ref, o_ref):
    hd = ATTN_HEAD_DIM
    k = k_ref[...]
    v = v_ref[...]
    for g in range(ATTN_GROUPS):
        sl = slice(g * hd, (g + 1) * hd)
        s = lax.dot_general(q_ref[:, sl], k, _NT, preferred_element_type=F32)
        p = jnp.exp(s - jnp.max(s, axis=-1, keepdims=True))
        l = jnp.sum(p, axis=-1, keepdims=True)
        o = jnp.dot(p.astype(BF16), v, preferred_element_type=F32)
        o_ref[:, sl] = (o / l).astype(o_ref.dtype)


def attn_core(qk, v, batch, seq, *, tq=512):
    hd, g = ATTN_HEAD_DIM, ATTN_GROUPS
    nq = seq // tq
    return pl.pallas_call(
        _attn_kernel,
        grid=(batch, ATTN_KV_HEADS, nq),
        in_specs=[
            pl.BlockSpec((tq, g * hd), lambda b, kv, i: (b * nq + i, kv)),
            pl.BlockSpec((seq, hd), lambda b, kv, i: (b, ATTN_Q_HEADS + kv)),
            pl.BlockSpec((seq, hd), lambda b, kv, i: (b, kv)),
        ],
        out_specs=pl.BlockSpec((tq, g * hd), lambda b, kv, i: (b * nq + i, kv)),
        out_shape=jax.ShapeDtypeStruct((batch * seq, ATTN_Q_HEADS * hd), BF16),
        compiler_params=_params("parallel", "parallel", "arbitrary"),
    )(qk, qk, v)


def _rope_tables(seq):
    rows = seq // GRID_W
    row = jnp.repeat(jnp.arange(rows, dtype=F32), GRID_W)
    col = jnp.tile(jnp.arange(GRID_W, dtype=F32), rows)
    n_freq = ATTN_HEAD_DIM // 4
    freqs = ROPE_THETA ** (-jnp.arange(n_freq, dtype=F32) / n_freq)
    ang = jnp.concatenate([row[:, None] * freqs, col[:, None] * freqs], axis=-1)
    cos, sin = jnp.cos(ang), jnp.sin(ang)
    return jnp.concatenate([cos, cos], axis=-1), jnp.concatenate([-sin, sin], axis=-1)


def attn_mixer(x2d, ln, w_in, q_norm, k_norm, w_out, batch, seq):
    hd = ATTN_HEAD_DIM
    n_qk_heads = ATTN_Q_HEADS + ATTN_KV_HEADS
    n_qk = n_qk_heads * hd
    perm = jnp.concatenate([jnp.arange(0, hd, 2), jnp.arange(1, hd, 2)])
    d_model = w_in.shape[0]
    w_qk = w_in[:, :n_qk].reshape(d_model, n_qk_heads, hd)[:, :, perm].reshape(d_model, n_qk).astype(BF16)
    w_v = w_in[:, n_qk:].astype(BF16)
    gain = jnp.concatenate([jnp.tile(q_norm[perm], ATTN_Q_HEADS), jnp.tile(k_norm[perm], ATTN_KV_HEADS)])
    mult = jnp.concatenate([jnp.full((ATTN_Q_HEADS * hd,), hd ** -0.5, F32),
                            jnp.ones((ATTN_KV_HEADS * hd,), F32)])
    cos_t, sin_t = _rope_tables(seq)
    qk = qk_proj(x2d, ln, w_qk, gain.reshape(1, n_qk).astype(F32), mult.reshape(1, n_qk), cos_t, sin_t, seq)
    v = norm_matmul(x2d, ln, w_v, BF16)
    o = attn_core(qk, v, batch, seq)
    return matmul_residual(o, w_out.astype(BF16), x2d)


_PEER_CAND = [(i, j) for i in range(PEER_TOPK) for j in range(min(PEER_TOPK, (PEER_TOPK + 1) // (i + 1)))]
_PEER_CAND_ROWS = -(-len(_PEER_CAND) // 8) * 8


def _top_values(x, n):
    vals = []
    for _ in range(n):
        m = jnp.max(x, axis=0, keepdims=True)
        vals.append(m)
        x = jnp.where(x == m, -jnp.inf, x)
    return vals


def _sorting_network(n):
    pairs = []
    p = 1
    while p < n:
        k = p
        while k >= 1:
            for j in range(k % p, n - k, 2 * k):
                for i in range(min(k, n - j - k)):
                    if (i + j) // (2 * p) == (i + j + k) // (2 * p):
                        pairs.append((i + j, i + j + k))
            k //= 2
        p *= 2
    return pairs


def _top_values_of_keys(x, n):
    slabs = [x[r:r + 8, :] for r in range(0, x.shape[0], 8)]
    for i, j in _sorting_network(len(slabs)):
        slabs[i], slabs[j] = jnp.maximum(slabs[i], slabs[j]), jnp.minimum(slabs[i], slabs[j])
    vals = []
    for k in range(n):
        m = jnp.max(slabs[0], axis=0, keepdims=True)
        vals.append(m)
        hit = slabs[0] == m
        for r in range(min(n - 1 - k, len(slabs))):
            below = slabs[r + 1] if r + 1 < len(slabs) else -jnp.inf
            slabs[r] = jnp.where(hit, below, slabs[r])
    return vals


def _peer_route_kernel(q_ref, k1_ref, k2_ref, thr_ref, c_ref, e2_ref, s2_ref, cand_ref):
    nk, half, topk = PEER_N_KEYS, PEER_HALF, PEER_TOPK
    k1 = k1_ref[...]
    k2 = k2_ref[...]
    tm = q_ref.shape[0]
    cand_ref[...] = jnp.full(cand_ref.shape, -jnp.inf, F32)
    for h in range(PEER_HEADS):
        qa = q_ref[:, (2 * h) * half:(2 * h + 1) * half]
        qb = q_ref[:, (2 * h + 1) * half:(2 * h + 2) * half]
        s1 = lax.dot_general(k1, qa, _NT, preferred_element_type=F32)
        s2 = lax.dot_general(k2, qb, _NT, preferred_element_type=F32)
        t1 = _top_values_of_keys(s1, topk + 1)
        t2 = _top_values_of_keys(s2, topk + 1)
        for r, (i, j) in enumerate(_PEER_CAND):
            cand_ref[r:r + 1, :] = t1[i] + t2[j]
        c = _top_values(cand_ref[...], topk + 1)
        c_next = jnp.maximum(c[topk], jnp.maximum(t1[topk] + t2[0], t1[0] + t2[topk]))
        thr = 0.5 * (c[topk - 1] + c_next)
        z = jnp.zeros((1, tm), F32)
        for kk in range(topk):
            z = z + jnp.exp(c[kk] - c[0])
        rows = slice(h * nk, (h + 1) * nk)
        e2 = jnp.exp(s2 - t2[0])
        for lc in range(tm // LANES):
            cols = slice(lc * LANES, (lc + 1) * LANES)
            e2_ref[lc, rows, :] = e2[:, cols]
            s2_ref[lc, rows, :] = s2[:, cols]
        n_sub = thr_ref.shape[1] // PEER_HEADS
        thr_a = thr - s1
        c_a = jnp.exp(s1 - t1[0]) / z
        for blk in range(nk // n_sub):
            thr_ref[blk, h * n_sub:(h + 1) * n_sub, :] = thr_a[blk * n_sub:(blk + 1) * n_sub, :]
            c_ref[blk, h * n_sub:(h + 1) * n_sub, :] = c_a[blk * n_sub:(blk + 1) * n_sub, :]


def peer_route(q, k1, k2, n_sub, *, tm=512):
    t, d = q.shape
    rows = PEER_HEADS * PEER_N_KEYS
    blocks = PEER_N_KEYS // n_sub
    a_spec = pl.BlockSpec((blocks, PEER_HEADS * n_sub, tm), lambda i: (0, 0, i))
    a_shape = jax.ShapeDtypeStruct((blocks, PEER_HEADS * n_sub, t), F32)
    b_spec = pl.BlockSpec((tm // LANES, rows, LANES), lambda i: (i, 0, 0))
    b_shape = jax.ShapeDtypeStruct((t // LANES, rows, LANES), F32)
    return pl.pallas_call(
        _peer_route_kernel,
        grid=(t // tm,),
        in_specs=[
            pl.BlockSpec((tm, d), lambda i: (i, 0)),
            pl.BlockSpec((PEER_N_KEYS, PEER_HALF), lambda i: (0, 0)),
            pl.BlockSpec((PEER_N_KEYS, PEER_HALF), lambda i: (0, 0)),
        ],
        out_specs=[a_spec, a_spec, b_spec, b_spec],
        out_shape=[a_shape, a_shape, b_shape, b_shape],
        scratch_shapes=[pltpu.VMEM((_PEER_CAND_ROWS, tm), F32)],
        compiler_params=_params("parallel"),
    )(q, k1, k2)


def _gelu_tanh(x):
    return 0.5 * x * (1.0 + jnp.tanh(math.sqrt(2.0 / math.pi) * (x + 0.044715 * (x * x * x))))


def _peer_expert_kernel(xt_ref, u_ref, vt_ref, thr_ref, c_ref, e2_ref, s2_ref, res_ref, zero_ref, o_ref,
                        acc_ref, w0_ref, w1_ref, g_ref, *, n_blocks):
    e = pl.program_id(1)
    nk = PEER_N_KEYS
    eb, d = u_ref.shape
    tt = xt_ref.shape[1]
    n_sub = eb // nk
    lane_tiles = tt // LANES
    n_stages = eb // PEER_ACT_CHUNK
    keys_per_stage = PEER_ACT_CHUNK // nk
    out_per_stage = d // PEER_OUT_CHUNK // n_stages

    def gate_tiles(a):
        zeros = []
        sub = PEER_GATE_ROWS
        for lc in range(lane_tiles):
            cols = slice(lc * LANES, (lc + 1) * LANES)
            folded = None
            for b0 in range(0, nk, sub):
                gate = jnp.zeros((sub, LANES), F32)
                for h in range(PEER_HEADS):
                    r = h * n_sub + a
                    rows = slice(h * nk + b0, h * nk + b0 + sub)
                    gate = gate + jnp.where(s2_ref[lc, rows, :] >= thr_ref[0, r:r + 1, cols],
                                            e2_ref[lc, rows, :] * c_ref[0, r:r + 1, cols], 0.0)
                g_ref[a * nk + b0:a * nk + b0 + sub, cols] = gate
                for r in range(0, sub, 8):
                    folded = gate[r:r + 8, :] if folded is None else folded + gate[r:r + 8, :]
            zeros.append(pltpu.bitcast(folded, jnp.uint32) & zero_ref[...])
        return zeros

    def paced(lhs, zeros):
        if not zeros:
            return lhs
        z = functools.reduce(jnp.bitwise_or, zeros)
        words = pltpu.bitcast(lhs, jnp.uint32)
        words = words | jnp.tile(z, (words.shape[0] // z.shape[0], words.shape[1] // z.shape[1]))
        return pltpu.bitcast(words, lhs.dtype)

    def run(w_old_ref, w_new_ref):
        for s in range(n_stages):
            zeros = []
            if w_new_ref is not None:
                for a in range(s * keys_per_stage, (s + 1) * keys_per_stage):
                    zeros += gate_tiles(a)
            if w_old_ref is not None:
                for k in range(s * out_per_stage, (s + 1) * out_per_stage):
                    rows = slice(k * PEER_OUT_CHUNK, (k + 1) * PEER_OUT_CHUNK)
                    share = len(zeros) // out_per_stage
                    lhs = paced(vt_ref[rows, :], zeros[(k % out_per_stage) * share:(k % out_per_stage + 1) * share])
                    acc_ref[rows, :] += jnp.dot(lhs, w_old_ref[...], preferred_element_type=F32)
            if w_new_ref is not None:
                rows = slice(s * PEER_ACT_CHUNK, (s + 1) * PEER_ACT_CHUNK)
                act = _gelu_tanh(jnp.dot(paced(u_ref[rows, :], zeros), xt_ref[...], preferred_element_type=F32))
                w_new_ref[rows, :] = (g_ref[rows, :] * act).astype(BF16)

    @pl.when(e == 0)
    def _():
        acc_ref[...] = jnp.zeros_like(acc_ref)
        run(None, w0_ref)

    is_mid = jnp.logical_and(e > 0, e < n_blocks)

    @pl.when(jnp.logical_and(is_mid, e % 2 == 1))
    def _():
        run(w0_ref, w1_ref)

    @pl.when(jnp.logical_and(is_mid, e % 2 == 0))
    def _():
        run(w1_ref, w0_ref)

    @pl.when(e == n_blocks)
    def _():
        run(w1_ref if n_blocks % 2 == 0 else w0_ref, None)
        o_ref[...] = res_ref[...] + acc_ref[...].T


def peer_experts(xt, u, vt, thr, c, e2, s2, res, *, tt=512):
    d, t = xt.shape
    n_exp = u.shape[0]
    eb = PEER_EXPERT_BLOCK
    n_blocks = n_exp // eb
    rows = PEER_HEADS * PEER_N_KEYS
    once = pl.Buffered(1)
    side_a = pl.BlockSpec((1, thr.shape[1], tt), lambda i, e: (jnp.minimum(e, n_blocks - 1), 0, i))
    side_b = pl.BlockSpec((tt // LANES, rows, LANES), lambda i, e: (i, 0, 0), pipeline_mode=once)
    kernel = functools.partial(_peer_expert_kernel, n_blocks=n_blocks)
    return pl.pallas_call(
        kernel,
        grid=(t // tt, n_blocks + 1),
        in_specs=[
            pl.BlockSpec((d, tt), lambda i, e: (0, i), pipeline_mode=once),
            pl.BlockSpec((eb, d), lambda i, e: (jnp.minimum(e, n_blocks - 1), 0)),
            pl.BlockSpec((d, eb), lambda i, e: (0, jnp.maximum(e - 1, 0))),
            side_a, side_a, side_b, side_b,
            pl.BlockSpec((tt, d), lambda i, e: (i, 0), pipeline_mode=once),
            pl.BlockSpec((8, LANES), lambda i, e: (0, 0)),
        ],
        out_specs=pl.BlockSpec((tt, d), lambda i, e: (i, 0)),
        out_shape=jax.ShapeDtypeStruct((t, d), F32),
        scratch_shapes=[pltpu.VMEM((d, tt), F32), pltpu.VMEM((eb, tt), BF16), pltpu.VMEM((eb, tt), BF16),
                        pltpu.VMEM((eb, tt), F32)],
        compiler_params=_params("parallel", "arbitrary"),
    )(xt, u, vt, thr, c, e2, s2, res, jnp.zeros((8, LANES), jnp.uint32))


def peer_mixer(x2d, ln, wq, k1, k2, u, v):
    q, xt = norm_matmul(x2d, ln, wq.astype(BF16), BF16, with_transposed_norm=True)
    thr, c, e2, s2 = peer_route(q, k1.astype(BF16), k2.astype(BF16), PEER_EXPERT_BLOCK // PEER_N_KEYS)
    return peer_experts(xt, u.astype(BF16), v.T.astype(BF16), thr, c, e2, s2, x2d)


def kernel(x, ln_mix, ln_ffn, mlstm_w_in, mlstm_gate_bias, mlstm_head_norm, mlstm_w_out,
           attn_w_in, attn_q_norm, attn_k_norm, attn_w_out,
           peer_wq, peer_k1, peer_k2, peer_u, peer_v):
    batch, seq, d_model = x.shape
    depth = ln_mix.shape[0]
    x2d = x.reshape(batch * seq, d_model)
    for layer in range(depth):
        j = layer // 2
        if layer % 2 == 0:
            x2d = mlstm_mixer(x2d, ln_mix[layer], mlstm_w_in[j], mlstm_gate_bias[j], mlstm_head_norm[j],
                              mlstm_w_out[j], batch, seq)
        else:
            x2d = attn_mixer(x2d, ln_mix[layer], attn_w_in[j], attn_q_norm[j], attn_k_norm[j],
                             attn_w_out[j], batch, seq)
        x2d = peer_mixer(x2d, ln_ffn[layer], peer_wq[layer], peer_k1[layer], peer_k2[layer],
                         peer_u[layer], peer_v[layer])
    return x2d.reshape(batch, seq, d_model)
```

```python
import functools
import math

import jax
import jax.numpy as jnp
from jax import lax
from jax.experimental import pallas as pl
from jax.experimental.pallas import tpu as pltpu

NORM_EPS = 1e-6

MLSTM_HEADS = 4
MLSTM_QK_DIM = 256
MLSTM_V_DIM = 512
MLSTM_BLOCKS = 2

ATTN_HEAD_DIM = 128
ATTN_Q_HEADS = 16
ATTN_KV_HEADS = 4
ATTN_GROUPS = ATTN_Q_HEADS // ATTN_KV_HEADS
GRID_W = 64
ROPE_THETA = 10000.0

PEER_HEADS = 8
PEER_N_KEYS = 128
PEER_TOPK = 16
PEER_HALF = 128
PEER_EXPERT_BLOCK = 1024
PEER_OUT_CHUNK = 256
PEER_ACT_CHUNK = 256
PEER_GATE_ROWS = 32

LANES = 128
VMEM_LIMIT_BYTES = 56 * 1024 * 1024

BF16 = jnp.bfloat16
F32 = jnp.float32

_NT = (((1,), (1,)), ((), ()))


def _params(*sem):
    return pltpu.CompilerParams(dimension_semantics=sem, vmem_limit_bytes=VMEM_LIMIT_BYTES)


def _rms(x, g):
    return x * lax.rsqrt(jnp.mean(x * x, axis=-1, keepdims=True) + NORM_EPS) * g


def _norm_matmul_kernel(x_ref, g_ref, w_ref, o_ref, xn_ref):
    @pl.when(pl.program_id(1) == 0)
    def _():
        xn_ref[...] = _rms(x_ref[...], g_ref[...]).astype(BF16)

    o_ref[...] = jnp.dot(xn_ref[...], w_ref[...], preferred_element_type=F32).astype(o_ref.dtype)


def _norm_matmul_t_kernel(x_ref, g_ref, w_ref, o_ref, xt_ref, xn_ref):
    @pl.when(pl.program_id(1) == 0)
    def _():
        y = _rms(x_ref[...], g_ref[...])
        xn_ref[...] = y.astype(BF16)
        xt_ref[...] = y.T.astype(BF16)

    o_ref[...] = jnp.dot(xn_ref[...], w_ref[...], preferred_element_type=F32).astype(o_ref.dtype)


def norm_matmul(x, g, w, out_dtype, *, tm=1024, tn=1024, with_transposed_norm=False):
    m, k = x.shape
    n = w.shape[1]
    tn = min(tn, n)
    grid = (m // tm, n // tn)
    in_specs = [
        pl.BlockSpec((tm, k), lambda i, j: (i, 0)),
        pl.BlockSpec((1, k), lambda i, j: (0, 0)),
        pl.BlockSpec((k, tn), lambda i, j: (0, j)),
    ]
    o_spec = pl.BlockSpec((tm, tn), lambda i, j: (i, j))
    scratch = [pltpu.VMEM((tm, k), BF16)]
    if with_transposed_norm:
        return pl.pallas_call(
            _norm_matmul_t_kernel,
            grid=grid, in_specs=in_specs,
            out_specs=[o_spec, pl.BlockSpec((k, tm), lambda i, j: (0, i))],
            out_shape=[jax.ShapeDtypeStruct((m, n), out_dtype), jax.ShapeDtypeStruct((k, m), BF16)],
            scratch_shapes=scratch,
            compiler_params=_params("parallel", "arbitrary"),
        )(x, g.reshape(1, k), w)
    return pl.pallas_call(
        _norm_matmul_kernel,
        grid=grid, in_specs=in_specs, out_specs=o_spec,
        out_shape=jax.ShapeDtypeStruct((m, n), out_dtype),
        scratch_shapes=scratch,
        compiler_params=_params("parallel", "arbitrary"),
    )(x, g.reshape(1, k), w)


def _matmul_residual_kernel(x_ref, w_ref, r_ref, o_ref):
    o_ref[...] = r_ref[...] + jnp.dot(x_ref[...], w_ref[...], preferred_element_type=F32)


def matmul_residual(x, w, res, *, tm=1024, tn=1024):
    m, k = x.shape
    n = w.shape[1]
    return pl.pallas_call(
        _matmul_residual_kernel,
        grid=(m // tm, n // tn),
        in_specs=[
            pl.BlockSpec((tm, k), lambda i, j: (i, 0)),
            pl.BlockSpec((k, tn), lambda i, j: (0, j)),
            pl.BlockSpec((tm, tn), lambda i, j: (i, j)),
        ],
        out_specs=pl.BlockSpec((tm, tn), lambda i, j: (i, j)),
        out_shape=jax.ShapeDtypeStruct((m, n), F32),
        compiler_params=_params("parallel", "arbitrary"),
    )(x, w, res)


def _scan_lanes(x, op, fill, is_suffix):
    s = x.shape[-1]
    lane = lax.broadcasted_iota(jnp.int32, x.shape, 1)
    d = 1
    while d < s:
        from_left = jnp.where(lane >= d, pltpu.roll(x, d, axis=1), fill)
        from_right = jnp.where(lane < s - d, pltpu.roll(x, s - d, axis=1), fill)
        x = op(x, jnp.where(is_suffix, from_right, from_left))
        d *= 2
    return x


def _gate_scan_kernel(ig_ref, fg_ref, bi_ref, bf_ref, a_ref, m_ref, e_ref):
    ig = ig_ref[0] + bi_ref[...]
    fg = fg_ref[0] + bf_ref[...]
    is_suffix = lax.broadcasted_iota(jnp.int32, ig.shape, 0) >= MLSTM_HEADS
    logsig = jnp.minimum(fg, 0.0) - jnp.log1p(jnp.exp(-jnp.abs(fg)))
    fcum = _scan_lanes(logsig, jnp.add, 0.0, is_suffix)
    a = ig - fcum
    mx = jnp.maximum(_scan_lanes(a, jnp.maximum, -jnp.inf, is_suffix), 0.0)
    a_ref[0] = a
    m_ref[0] = mx
    e_ref[0] = jnp.exp(-(fcum + mx))


def gate_scan(ig, fg, bi, bf):
    b, r, s = ig.shape
    g_spec = pl.BlockSpec((1, r, s), lambda i: (i, 0, 0))
    b_spec = pl.BlockSpec((r, 1), lambda i: (0, 0))
    o_shape = jax.ShapeDtypeStruct((b, r, s), F32)
    return pl.pallas_call(
        _gate_scan_kernel,
        grid=(b,),
        in_specs=[g_spec, g_spec, b_spec, b_spec],
        out_specs=[g_spec] * 3,
        out_shape=[o_shape] * 3,
        compiler_params=_params("parallel"),
    )(ig, fg, bi, bf)


def _mlstm_kernel(q_ref, k_ref, v_ref, o_ref, row_ref, col_ref, hn_ref, out_ref, *, tq):
    i = pl.program_id(2)
    j = 1 - i
    other_is_past = i > j
    q = q_ref[...]
    col = col_ref[0, 0]
    m_f, m_b, e_f, e_b = col[:, 0:1], col[:, 1:2], col[:, 2:3], col[:, 3:4]
    scale = MLSTM_QK_DIM ** -0.5

    def scores(blk):
        start = pl.multiple_of(blk * tq, tq)
        kb = k_ref[pl.ds(start, tq), :]
        vb = v_ref[pl.ds(start, tq), :]
        return lax.dot_general(q, kb, _NT, preferred_element_type=F32) * scale, vb

    def weighted(p, vb):
        return jnp.dot(p.astype(BF16), vb, preferred_element_type=F32), jnp.sum(p, axis=-1, keepdims=True)

    s_o, v_o = scores(j)
    rows_o = row_ref[0, 0, j]
    a_o = jnp.where(other_is_past, rows_o[0:1, :], rows_o[1:2, :])
    m_o = jnp.where(other_is_past, m_f, m_b)
    num_o, den_o = weighted(s_o * jnp.exp(a_o - m_o), v_o)

    s_d, v_d = scores(i)
    rows_d = row_ref[0, 0, i]
    t_idx = lax.broadcasted_iota(jnp.int32, (tq, tq), 0)
    s_idx = lax.broadcasted_iota(jnp.int32, (tq, tq), 1)
    num_f, den_f = weighted(s_d * jnp.where(s_idx <= t_idx, jnp.exp(rows_d[0:1, :] - m_f), 0.0), v_d)
    num_b, den_b = weighted(s_d * jnp.where(s_idx >= t_idx, jnp.exp(rows_d[1:2, :] - m_b), 0.0), v_d)

    num_f = num_f + jnp.where(other_is_past, num_o, 0.0)
    den_f = den_f + jnp.where(other_is_past, den_o, 0.0)
    num_b = num_b + jnp.where(other_is_past, 0.0, num_o)
    den_b = den_b + jnp.where(other_is_past, 0.0, den_o)

    h = num_f / jnp.maximum(jnp.abs(den_f), e_f) + num_b / jnp.maximum(jnp.abs(den_b), e_b)
    h = _rms(h, hn_ref[0])
    out_ref[...] = (h * jax.nn.sigmoid(o_ref[...].astype(F32))).astype(out_ref.dtype)


def mlstm_core(proj, rowp, colp, head_norm, batch, seq):
    h, dk, dv = MLSTM_HEADS, MLSTM_QK_DIM, MLSTM_V_DIM
    nq = MLSTM_BLOCKS
    assert seq % nq == 0
    tq = seq // nq
    k_col0 = h * dk // dk
    v_col0 = 2 * h * dk // dv
    o_col0 = (2 * h * dk + h * dv) // dv
    kernel = functools.partial(_mlstm_kernel, tq=tq)
    return pl.pallas_call(
        kernel,
        grid=(batch, h, nq),
        in_specs=[
            pl.BlockSpec((tq, dk), lambda b, hh, i: (b * nq + i, hh)),
            pl.BlockSpec((seq, dk), lambda b, hh, i: (b, k_col0 + hh)),
            pl.BlockSpec((seq, dv), lambda b, hh, i: (b, v_col0 + hh)),
            pl.BlockSpec((tq, dv), lambda b, hh, i: (b * nq + i, o_col0 + hh)),
            pl.BlockSpec((1, 1, nq, 8, tq), lambda b, hh, i: (b, hh, 0, 0, 0)),
            pl.BlockSpec((1, 1, tq, 8), lambda b, hh, i: (b, hh, i, 0)),
            pl.BlockSpec((1, 1, dv), lambda b, hh, i: (hh, 0, 0)),
        ],
        out_specs=pl.BlockSpec((tq, dv), lambda b, hh, i: (b * nq + i, hh)),
        out_shape=jax.ShapeDtypeStruct((batch * seq, h * dv), BF16),
        compiler_params=_params("parallel", "parallel", "arbitrary"),
    )(proj, proj, proj, proj, rowp, colp, head_norm.reshape(h, 1, dv))


def mlstm_mixer(x2d, ln, w_in, gate_bias, head_norm, w_out, batch, seq):
    h = MLSTM_HEADS
    n_main = 2 * h * MLSTM_QK_DIM + 2 * h * MLSTM_V_DIM
    w_main = w_in[:, :n_main].astype(BF16)
    w_gate = jnp.pad(w_in[:, n_main:], ((0, 0), (0, LANES - 4 * h))).astype(BF16)
    proj = norm_matmul(x2d, ln, w_main, BF16)
    gates = norm_matmul(x2d, ln, w_gate, F32)[:, :4 * h]
    gates_t = gates.reshape(batch, seq, 4, h).transpose(0, 2, 3, 1)
    bias = gate_bias.astype(F32)
    a, mx, en = gate_scan(gates_t[:, 0::2].reshape(batch, 2 * h, seq), gates_t[:, 1::2].reshape(batch, 2 * h, seq),
                          bias[0::2].reshape(2 * h, 1), bias[1::2].reshape(2 * h, 1))
    nk = MLSTM_BLOCKS
    tq = seq // nk
    rowp = a.reshape(batch, 2, h, nk, tq).transpose(0, 2, 3, 1, 4)
    rowp = jnp.pad(rowp, ((0, 0), (0, 0), (0, 0), (0, 6), (0, 0)))
    colp = jnp.concatenate([mx.reshape(batch, 2, h, seq), en.reshape(batch, 2, h, seq)], axis=1)
    colp = jnp.pad(colp.transpose(0, 2, 3, 1), ((0, 0), (0, 0), (0, 0), (0, 4)))
    hg = mlstm_core(proj, rowp, colp, head_norm, batch, seq)
    return matmul_residual(hg, w_out.astype(BF16), x2d)


def _qk_proj_kernel(x_ref, g_ref, w_ref, gain_ref, mult_ref, cos_ref, sin_ref, o_ref, xn_ref):
    @pl.when(pl.program_id(1) == 0)
    def _():
        xn_ref[...] = _rms(x_ref[...], g_ref[...]).astype(BF16)

    acc = jnp.dot(xn_ref[...], w_ref[...], preferred_element_type=F32)
    hd = ATTN_HEAD_DIM
    cos = cos_ref[...]
    sin = sin_ref[...]
    for c in range(acc.shape[1] // hd):
        sl = slice(c * hd, (c + 1) * hd)
        y = _rms(acc[:, sl], gain_ref[:, sl])
        y = y * cos + pltpu.roll(y, hd // 2, axis=1) * sin
        o_ref[:, sl] = (y * mult_ref[:, sl]).astype(o_ref.dtype)


def qk_proj(x, g, w, gain, mult, cos_t, sin_t, seq, *, tm=512, tn=2560):
    m, k = x.shape
    n = w.shape[1]
    nseq = seq // tm
    return pl.pallas_call(
        _qk_proj_kernel,
        grid=(m // tm, n // tn),
        in_specs=[
            pl.BlockSpec((tm, k), lambda i, j: (i, 0)),
            pl.BlockSpec((1, k), lambda i, j: (0, 0)),
            pl.BlockSpec((k, tn), lambda i, j: (0, j)),
            pl.BlockSpec((1, tn), lambda i, j: (0, j)),
            pl.BlockSpec((1, tn), lambda i, j: (0, j)),
            pl.BlockSpec((tm, ATTN_HEAD_DIM), lambda i, j: (i % nseq, 0)),
            pl.BlockSpec((tm, ATTN_HEAD_DIM), lambda i, j: (i % nseq, 0)),
        ],
        out_specs=pl.BlockSpec((tm, tn), lambda i, j: (i, j)),
        out_shape=jax.ShapeDtypeStruct((m, n), BF16),
        scratch_shapes=[pltpu.VMEM((tm, k), BF16)],
        compiler_params=_params("parallel", "arbitrary"),
    )(x, g.reshape(1, k), w, gain, mult, cos_t, sin_t)


def _attn_kernel(q_ref, k_ref, v_ref, o_ref):
    hd = ATTN_HEAD_DIM
    k = k_ref[...]
    v = v_ref[...]
    for g in range(ATTN_GROUPS):
        sl = slice(g * hd, (g + 1) * hd)
        s = lax.dot_general(q_ref[:, sl], k, _NT, preferred_element_type=F32)
        p = jnp.exp(s - jnp.max(s, axis=-1, keepdims=True))
        l = jnp.sum(p, axis=-1, keepdims=True)
        o = jnp.dot(p.astype(BF16), v, preferred_element_type=F32)
        o_ref[:, sl] = (o / l).astype(o_ref.dtype)


def attn_core(qk, v, batch, seq, *, tq=512):
    hd, g = ATTN_HEAD_DIM, ATTN_GROUPS
    nq = seq // tq
    return pl.pallas_call(
        _attn_kernel,
        grid=(batch, ATTN_KV_HEADS, nq),
        in_specs=[
            pl.BlockSpec((tq, g * hd), lambda b, kv, i: (b * nq + i, kv)),
            pl.BlockSpec((seq, hd), lambda b, kv, i: (b, ATTN_Q_HEADS + kv)),
            pl.BlockSpec((seq, hd), lambda b, kv, i: (b, kv)),
        ],
        out_specs=pl.BlockSpec((tq, g * hd), lambda b, kv, i: (b * nq + i, kv)),
        out_shape=jax.ShapeDtypeStruct((batch * seq, ATTN_Q_HEADS * hd), BF16),
        compiler_params=_params("parallel", "parallel", "arbitrary"),
    )(qk, qk, v)


def _rope_tables(seq):
    rows = seq // GRID_W
    row = jnp.repeat(jnp.arange(rows, dtype=F32), GRID_W)
    col = jnp.tile(jnp.arange(GRID_W, dtype=F32), rows)
    n_freq = ATTN_HEAD_DIM // 4
    freqs = ROPE_THETA ** (-jnp.arange(n_freq, dtype=F32) / n_freq)
    ang = jnp.concatenate([row[:, None] * freqs, col[:, None] * freqs], axis=-1)
    cos, sin = jnp.cos(ang), jnp.sin(ang)
    return jnp.concatenate([cos, cos], axis=-1), jnp.concatenate([-sin, sin], axis=-1)


def attn_mixer(x2d, ln, w_in, q_norm, k_norm, w_out, batch, seq):
    hd = ATTN_HEAD_DIM
    n_qk_heads = ATTN_Q_HEADS + ATTN_KV_HEADS
    n_qk = n_qk_heads * hd
    perm = jnp.concatenate([jnp.arange(0, hd, 2), jnp.arange(1, hd, 2)])
    d_model = w_in.shape[0]
    w_qk = w_in[:, :n_qk].reshape(d_model, n_qk_heads, hd)[:, :, perm].reshape(d_model, n_qk).astype(BF16)
    w_v = w_in[:, n_qk:].astype(BF16)
    gain = jnp.concatenate([jnp.tile(q_norm[perm], ATTN_Q_HEADS), jnp.tile(k_norm[perm], ATTN_KV_HEADS)])
    mult = jnp.concatenate([jnp.full((ATTN_Q_HEADS * hd,), hd ** -0.5, F32),
                            jnp.ones((ATTN_KV_HEADS * hd,), F32)])
    cos_t, sin_t = _rope_tables(seq)
    qk = qk_proj(x2d, ln, w_qk, gain.reshape(1, n_qk).astype(F32), mult.reshape(1, n_qk), cos_t, sin_t, seq)
    v = norm_matmul(x2d, ln, w_v, BF16)
    o = attn_core(qk, v, batch, seq)
    return matmul_residual(o, w_out.astype(BF16), x2d)


_PEER_CAND = [(i, j) for i in range(PEER_TOPK) for j in range(min(PEER_TOPK, (PEER_TOPK + 1) // (i + 1)))]
_PEER_CAND_ROWS = -(-len(_PEER_CAND) // 8) * 8


def _top_values(x, n):
    vals = []
    for _ in range(n):
        m = jnp.max(x, axis=0, keepdims=True)
        vals.append(m)
        x = jnp.where(x == m, -jnp.inf, x)
    return vals


def _sorting_network(n):
    pairs = []
    p = 1
    while p < n:
        k = p
        while k >= 1:
            for j in range(k % p, n - k, 2 * k):
                for i in range(min(k, n - j - k)):
                    if (i + j) // (2 * p) == (i + j + k) // (2 * p):
                        pairs.append((i + j, i + j + k))
            k //= 2
        p *= 2
    return pairs


def _top_values_of_keys(x, n):
    slabs = [x[r:r + 8, :] for r in range(0, x.shape[0], 8)]
    for i, j in _sorting_network(len(slabs)):
        slabs[i], slabs[j] = jnp.maximum(slabs[i], slabs[j]), jnp.minimum(slabs[i], slabs[j])
    vals = []
    for k in range(n):
        m = jnp.max(slabs[0], axis=0, keepdims=True)
        vals.append(m)
        hit = slabs[0] == m
        for r in range(min(n - 1 - k, len(slabs))):
            below = slabs[r + 1] if r + 1 < len(slabs) else -jnp.inf
            slabs[r] = jnp.where(hit, below, slabs[r])
    return vals


def _peer_route_kernel(q_ref, k1_ref, k2_ref, thr_ref, c_ref, e2_ref, s2_ref, cand_ref):
    nk, half, topk = PEER_N_KEYS, PEER_HALF, PEER_TOPK
    k1 = k1_ref[...]
    k2 = k2_ref[...]
    tm = q_ref.shape[0]
    cand_ref[...] = jnp.full(cand_ref.shape, -jnp.inf, F32)
    for h in range(PEER_HEADS):
        qa = q_ref[:, (2 * h) * half:(2 * h + 1) * half]
        qb = q_ref[:, (2 * h + 1) * half:(2 * h + 2) * half]
        s1 = lax.dot_general(k1, qa, _NT, preferred_element_type=F32)
        s2 = lax.dot_general(k2, qb, _NT, preferred_element_type=F32)
        t1 = _top_values_of_keys(s1, topk + 1)
        t2 = _top_values_of_keys(s2, topk + 1)
        for r, (i, j) in enumerate(_PEER_CAND):
            cand_ref[r:r + 1, :] = t1[i] + t2[j]
        c = _top_values(cand_ref[...], topk + 1)
        c_next = jnp.maximum(c[topk], jnp.maximum(t1[topk] + t2[0], t1[0] + t2[topk]))
        thr = 0.5 * (c[topk - 1] + c_next)
        z = jnp.zeros((1, tm), F32)
        for kk in range(topk):
            z = z + jnp.exp(c[kk] - c[0])
        rows = slice(h * nk, (h + 1) * nk)
        e2 = jnp.exp(s2 - t2[0])
        for lc in range(tm // LANES):
            cols = slice(lc * LANES, (lc + 1) * LANES)
            e2_ref[lc, rows, :] = e2[:, cols]
            s2_ref[lc, rows, :] = s2[:, cols]
        n_sub = thr_ref.shape[1] // PEER_HEADS
        thr_a = thr - s1
        c_a = jnp.exp(s1 - t1[0]) / z
        for blk in range(nk // n_sub):
            thr_ref[blk, h * n_sub:(h + 1) * n_sub, :] = thr_a[blk * n_sub:(blk + 1) * n_sub, :]
            c_ref[blk, h * n_sub:(h + 1) * n_sub, :] = c_a[blk * n_sub:(blk + 1) * n_sub, :]


def peer_route(q, k1, k2, n_sub, *, tm=512):
    t, d = q.shape
    rows = PEER_HEADS * PEER_N_KEYS
    blocks = PEER_N_KEYS // n_sub
    a_spec = pl.BlockSpec((blocks, PEER_HEADS * n_sub, tm), lambda i: (0, 0, i))
    a_shape = jax.ShapeDtypeStruct((blocks, PEER_HEADS * n_sub, t), F32)
    b_spec = pl.BlockSpec((tm // LANES, rows, LANES), lambda i: (i, 0, 0))
    b_shape = jax.ShapeDtypeStruct((t // LANES, rows, LANES), F32)
    return pl.pallas_call(
        _peer_route_kernel,
        grid=(t // tm,),
        in_specs=[
            pl.BlockSpec((tm, d), lambda i: (i, 0)),
            pl.BlockSpec((PEER_N_KEYS, PEER_HALF), lambda i: (0, 0)),
            pl.BlockSpec((PEER_N_KEYS, PEER_HALF), lambda i: (0, 0)),
        ],
        out_specs=[a_spec, a_spec, b_spec, b_spec],
        out_shape=[a_shape, a_shape, b_shape, b_shape],
        scratch_shapes=[pltpu.VMEM((_PEER_CAND_ROWS, tm), F32)],
        compiler_params=_params("parallel"),
    )(q, k1, k2)


def _gelu_tanh(x):
    return 0.5 * x * (1.0 + jnp.tanh(math.sqrt(2.0 / math.pi) * (x + 0.044715 * (x * x * x))))


def _peer_expert_kernel(xt_ref, u_ref, vt_ref, thr_ref, c_ref, e2_ref, s2_ref, res_ref, zero_ref, o_ref,
                        acc_ref, w0_ref, w1_ref, g_ref, *, n_blocks):
    e = pl.program_id(1)
    nk = PEER_N_KEYS
    eb, d = u_ref.shape
    tt = xt_ref.shape[1]
    n_sub = eb // nk
    lane_tiles = tt // LANES
    n_stages = eb // PEER_ACT_CHUNK
    keys_per_stage = PEER_ACT_CHUNK // nk
    out_per_stage = d // PEER_OUT_CHUNK // n_stages

    def gate_tiles(a):
        zeros = []
        sub = PEER_GATE_ROWS
        for lc in range(lane_tiles):
            cols = slice(lc * LANES, (lc + 1) * LANES)
            folded = None
            for b0 in range(0, nk, sub):
                gate = jnp.zeros((sub, LANES), F32)
                for h in range(PEER_HEADS):
                    r = h * n_sub + a
                    rows = slice(h * nk + b0, h * nk + b0 + sub)
                    gate = gate + jnp.where(s2_ref[lc, rows, :] >= thr_ref[0, r:r + 1, cols],
                                            e2_ref[lc, rows, :] * c_ref[0, r:r + 1, cols], 0.0)
                g_ref[a * nk + b0:a * nk + b0 + sub, cols] = gate
                for r in range(0, sub, 8):
                    folded = gate[r:r + 8, :] if folded is None else folded + gate[r:r + 8, :]
            zeros.append(pltpu.bitcast(folded, jnp.uint32) & zero_ref[...])
        return zeros

    def paced(lhs, zeros):
        if not zeros:
            return lhs
        z = functools.reduce(jnp.bitwise_or, zeros)
        words = pltpu.bitcast(lhs, jnp.uint32)
        words = words | jnp.tile(z, (words.shape[0] // z.shape[0], words.shape[1] // z.shape[1]))
        return pltpu.bitcast(words, lhs.dtype)

    def run(w_old_ref, w_new_ref):
        for s in range(n_stages):
            zeros = []
            if w_new_ref is not None:
                for a in range(s * keys_per_stage, (s + 1) * keys_per_stage):
                    zeros += gate_tiles(a)
            if w_old_ref is not None:
                for k in range(s * out_per_stage, (s + 1) * out_per_stage):
                    rows = slice(k * PEER_OUT_CHUNK, (k + 1) * PEER_OUT_CHUNK)
                    share = len(zeros) // out_per_stage
                    lhs = paced(vt_ref[rows, :], zeros[(k % out_per_stage) * share:(k % out_per_stage + 1) * share])
                    acc_ref[rows, :] += jnp.dot(lhs, w_old_ref[...], preferred_element_type=F32)
            if w_new_ref is not None:
                rows = slice(s * PEER_ACT_CHUNK, (s + 1) * PEER_ACT_CHUNK)
                act = _gelu_tanh(jnp.dot(paced(u_ref[rows, :], zeros), xt_ref[...], preferred_element_type=F32))
                w_new_ref[rows, :] = (g_ref[rows, :] * act).astype(BF16)

    @pl.when(e == 0)
    def _():
        acc_ref[...] = jnp.zeros_like(acc_ref)
        run(None, w0_ref)

    is_mid = jnp.logical_and(e > 0, e < n_blocks)

    @pl.when(jnp.logical_and(is_mid, e % 2 == 1))
    def _():
        run(w0_ref, w1_ref)

    @pl.when(jnp.logical_and(is_mid, e % 2 == 0))
    def _():
        run(w1_ref, w0_ref)

    @pl.when(e == n_blocks)
    def _():
        run(w1_ref if n_blocks % 2 == 0 else w0_ref, None)
        o_ref[...] = res_ref[...] + acc_ref[...].T


def peer_experts(xt, u, vt, thr, c, e2, s2, res, *, tt=512):
    d, t = xt.shape
    n_exp = u.shape[0]
    eb = PEER_EXPERT_BLOCK
    n_blocks = n_exp // eb
    rows = PEER_HEADS * PEER_N_KEYS
    once = pl.Buffered(1)
    side_a = pl.BlockSpec((1, thr.shape[1], tt), lambda i, e: (jnp.minimum(e, n_blocks - 1), 0, i))
    side_b = pl.BlockSpec((tt // LANES, rows, LANES), lambda i, e: (i, 0, 0), pipeline_mode=once)
    kernel = functools.partial(_peer_expert_kernel, n_blocks=n_blocks)
    return pl.pallas_call(
        kernel,
        grid=(t // tt, n_blocks + 1),
        in_specs=[
            pl.BlockSpec((d, tt), lambda i, e: (0, i), pipeline_mode=once),
            pl.BlockSpec((eb, d), lambda i, e: (jnp.minimum(e, n_blocks - 1), 0)),
            pl.BlockSpec((d, eb), lambda i, e: (0, jnp.maximum(e - 1, 0))),
            side_a, side_a, side_b, side_b,
            pl.BlockSpec((tt, d), lambda i, e: (i, 0), pipeline_mode=once),
            pl.BlockSpec((8, LANES), lambda i, e: (0, 0)),
        ],
        out_specs=pl.BlockSpec((tt, d), lambda i, e: (i, 0)),
        out_shape=jax.ShapeDtypeStruct((t, d), F32),
        scratch_shapes=[pltpu.VMEM((d, tt), F32), pltpu.VMEM((eb, tt), BF16), pltpu.VMEM((eb, tt), BF16),
                        pltpu.VMEM((eb, tt), F32)],
        compiler_params=_params("parallel", "arbitrary"),
    )(xt, u, vt, thr, c, e2, s2, res, jnp.zeros((8, LANES), jnp.uint32))


def _transpose_cast_kernel(x_ref, o_ref):
    o_ref[...] = x_ref[0].T.astype(o_ref.dtype)


def transpose_cast(x, layer, dtype, *, tr=1024):
    _, r, c = x.shape
    return pl.pallas_call(
        _transpose_cast_kernel,
        grid=(r // tr,),
        in_specs=[pl.BlockSpec((1, tr, c), lambda i: (layer, i, 0))],
        out_specs=pl.BlockSpec((c, tr), lambda i: (0, i)),
        out_shape=jax.ShapeDtypeStruct((c, r), dtype),
        compiler_params=_params("parallel"),
    )(x)


def _cast_kernel(x_ref, o_ref):
    o_ref[...] = x_ref[0].astype(o_ref.dtype)


def layer_cast(x, layer, dtype, *, tr=1024):
    _, r, c = x.shape
    return pl.pallas_call(
        _cast_kernel,
        grid=(r // tr,),
        in_specs=[pl.BlockSpec((1, tr, c), lambda i: (layer, i, 0))],
        out_specs=pl.BlockSpec((tr, c), lambda i: (i, 0)),
        out_shape=jax.ShapeDtypeStruct((r, c), dtype),
        compiler_params=_params("parallel"),
    )(x)


def peer_mixer(x2d, ln, wq, k1, k2, u_all, v_all, layer):
    q, xt = norm_matmul(x2d, ln, wq.astype(BF16), BF16, with_transposed_norm=True)
    thr, c, e2, s2 = peer_route(q, k1.astype(BF16), k2.astype(BF16), PEER_EXPERT_BLOCK // PEER_N_KEYS)
    return peer_experts(xt, layer_cast(u_all, layer, BF16), transpose_cast(v_all, layer, BF16), thr, c, e2, s2, x2d)


def kernel(x, ln_mix, ln_ffn, mlstm_w_in, mlstm_gate_bias, mlstm_head_norm, mlstm_w_out,
           attn_w_in, attn_q_norm, attn_k_norm, attn_w_out,
           peer_wq, peer_k1, peer_k2, peer_u, peer_v):
    batch, seq, d_model = x.shape
    depth = ln_mix.shape[0]
    x2d = x.reshape(batch * seq, d_model)
    for layer in range(depth):
        j = layer // 2
        if layer % 2 == 0:
            x2d = mlstm_mixer(x2d, ln_mix[layer], mlstm_w_in[j], mlstm_gate_bias[j], mlstm_head_norm[j],
                              mlstm_w_out[j], batch, seq)
        else:
            x2d = attn_mixer(x2d, ln_mix[layer], attn_w_in[j], attn_q_norm[j], attn_k_norm[j],
                             attn_w_out[j], batch, seq)
        x2d = peer_mixer(x2d, ln_ffn[layer], peer_wq[layer], peer_k1[layer], peer_k2[layer],
                         peer_u, peer_v, layer)
    return x2d.reshape(batch, seq, d_model)
```

```python
import functools
import math

import jax
import jax.numpy as jnp
from jax import lax
from jax.experimental import pallas as pl
from jax.experimental.pallas import tpu as pltpu

NORM_EPS = 1e-6

MLSTM_HEADS = 4
MLSTM_QK_DIM = 256
MLSTM_V_DIM = 512
MLSTM_BLOCKS = 2

ATTN_HEAD_DIM = 128
ATTN_Q_HEADS = 16
ATTN_KV_HEADS = 4
ATTN_GROUPS = ATTN_Q_HEADS // ATTN_KV_HEADS
GRID_W = 64
ROPE_THETA = 10000.0

PEER_HEADS = 8
PEER_N_KEYS = 128
PEER_TOPK = 16
PEER_HALF = 128
PEER_EXPERT_BLOCK = 1024
PEER_OUT_CHUNK = 256
PEER_ACT_CHUNK = 256
PEER_GATE_ROWS = 32

LANES = 128
VMEM_LIMIT_BYTES = 56 * 1024 * 1024

BF16 = jnp.bfloat16
F32 = jnp.float32

_NT = (((1,), (1,)), ((), ()))


def _params(*sem):
    return pltpu.CompilerParams(dimension_semantics=sem, vmem_limit_bytes=VMEM_LIMIT_BYTES)


def _rms(x, g):
    return x * lax.rsqrt(jnp.mean(x * x, axis=-1, keepdims=True) + NORM_EPS) * g


def _norm_matmul_kernel(x_ref, g_ref, w_ref, o_ref, xn_ref):
    @pl.when(pl.program_id(1) == 0)
    def _():
        xn_ref[...] = _rms(x_ref[...], g_ref[...]).astype(BF16)

    o_ref[...] = jnp.dot(xn_ref[...], w_ref[...], preferred_element_type=F32).astype(o_ref.dtype)


def _norm_matmul_t_kernel(x_ref, g_ref, w_ref, o_ref, xt_ref):
    y = _rms(x_ref[...], g_ref[...])
    xt_ref[...] = y.T.astype(BF16)
    o_ref[...] = jnp.dot(y.astype(BF16), w_ref[...], preferred_element_type=F32).astype(o_ref.dtype)


def norm_matmul(x, g, w, out_dtype, *, tm=1024, tn=1024, with_transposed_norm=False):
    m, k = x.shape
    n = w.shape[1]
    tn = min(tn, n)
    grid = (m // tm, n // tn)
    in_specs = [
        pl.BlockSpec((tm, k), lambda i, j: (i, 0)),
        pl.BlockSpec((1, k), lambda i, j: (0, 0)),
        pl.BlockSpec((k, tn), lambda i, j: (0, j)),
    ]
    o_spec = pl.BlockSpec((tm, tn), lambda i, j: (i, j))
    scratch = [pltpu.VMEM((tm, k), BF16)]
    if with_transposed_norm:
        assert grid[1] == 1
        return pl.pallas_call(
            _norm_matmul_t_kernel,
            grid=grid, in_specs=in_specs,
            out_specs=[o_spec, pl.BlockSpec((k, tm), lambda i, j: (0, i))],
            out_shape=[jax.ShapeDtypeStruct((m, n), out_dtype), jax.ShapeDtypeStruct((k, m), BF16)],
            compiler_params=_params("parallel", "arbitrary"),
        )(x, g.reshape(1, k), w)
    return pl.pallas_call(
        _norm_matmul_kernel,
        grid=grid, in_specs=in_specs, out_specs=o_spec,
        out_shape=jax.ShapeDtypeStruct((m, n), out_dtype),
        scratch_shapes=scratch,
        compiler_params=_params("parallel", "arbitrary"),
    )(x, g.reshape(1, k), w)


def _matmul_residual_kernel(x_ref, w_ref, r_ref, o_ref):
    o_ref[...] = r_ref[...] + jnp.dot(x_ref[...], w_ref[...], preferred_element_type=F32)


def matmul_residual(x, w, res, *, tm=512, tn=2048):
    m, k = x.shape
    n = w.shape[1]
    return pl.pallas_call(
        _matmul_residual_kernel,
        grid=(m // tm, n // tn),
        in_specs=[
            pl.BlockSpec((tm, k), lambda i, j: (i, 0)),
            pl.BlockSpec((k, tn), lambda i, j: (0, j)),
            pl.BlockSpec((tm, tn), lambda i, j: (i, j)),
        ],
        out_specs=pl.BlockSpec((tm, tn), lambda i, j: (i, j)),
        out_shape=jax.ShapeDtypeStruct((m, n), F32),
        compiler_params=_params("parallel", "arbitrary"),
    )(x, w, res)


def _scan_lanes(x, op, fill, is_suffix):
    s = x.shape[-1]
    lane = lax.broadcasted_iota(jnp.int32, x.shape, 1)
    d = 1
    while d < s:
        from_left = jnp.where(lane >= d, pltpu.roll(x, d, axis=1), fill)
        from_right = jnp.where(lane < s - d, pltpu.roll(x, s - d, axis=1), fill)
        x = op(x, jnp.where(is_suffix, from_right, from_left))
        d *= 2
    return x


def _gate_scan_kernel(ig_ref, fg_ref, bi_ref, bf_ref, a_ref, m_ref, e_ref):
    ig = ig_ref[0] + bi_ref[...]
    fg = fg_ref[0] + bf_ref[...]
    is_suffix = lax.broadcasted_iota(jnp.int32, ig.shape, 0) >= MLSTM_HEADS
    logsig = jnp.minimum(fg, 0.0) - jnp.log1p(jnp.exp(-jnp.abs(fg)))
    fcum = _scan_lanes(logsig, jnp.add, 0.0, is_suffix)
    a = ig - fcum
    mx = jnp.maximum(_scan_lanes(a, jnp.maximum, -jnp.inf, is_suffix), 0.0)
    a_ref[0] = a
    m_ref[0] = mx
    e_ref[0] = jnp.exp(-(fcum + mx))


def gate_scan(ig, fg, bi, bf):
    b, r, s = ig.shape
    g_spec = pl.BlockSpec((1, r, s), lambda i: (i, 0, 0))
    b_spec = pl.BlockSpec((r, 1), lambda i: (0, 0))
    o_shape = jax.ShapeDtypeStruct((b, r, s), F32)
    return pl.pallas_call(
        _gate_scan_kernel,
        grid=(b,),
        in_specs=[g_spec, g_spec, b_spec, b_spec],
        out_specs=[g_spec] * 3,
        out_shape=[o_shape] * 3,
        compiler_params=_params("parallel"),
    )(ig, fg, bi, bf)


def _mlstm_kernel(q_ref, k_ref, v_ref, o_ref, row_ref, col_ref, hn_ref, out_ref, *, tq):
    i = pl.program_id(2)
    j = 1 - i
    other_is_past = i > j
    q = q_ref[...]
    col = col_ref[0, 0]
    m_f, m_b, e_f, e_b = col[:, 0:1], col[:, 1:2], col[:, 2:3], col[:, 3:4]
    scale = MLSTM_QK_DIM ** -0.5

    def scores(blk):
        start = pl.multiple_of(blk * tq, tq)
        kb = k_ref[pl.ds(start, tq), :]
        vb = v_ref[pl.ds(start, tq), :]
        return lax.dot_general(q, kb, _NT, preferred_element_type=F32) * scale, vb

    def weighted(p, vb):
        return jnp.dot(p.astype(BF16), vb, preferred_element_type=F32), jnp.sum(p, axis=-1, keepdims=True)

    s_o, v_o = scores(j)
    rows_o = row_ref[0, 0, j]
    a_o = jnp.where(other_is_past, rows_o[0:1, :], rows_o[1:2, :])
    m_o = jnp.where(other_is_past, m_f, m_b)
    num_o, den_o = weighted(s_o * jnp.exp(a_o - m_o), v_o)

    s_d, v_d = scores(i)
    rows_d = row_ref[0, 0, i]
    t_idx = lax.broadcasted_iota(jnp.int32, (tq, tq), 0)
    s_idx = lax.broadcasted_iota(jnp.int32, (tq, tq), 1)
    num_f, den_f = weighted(s_d * jnp.where(s_idx <= t_idx, jnp.exp(rows_d[0:1, :] - m_f), 0.0), v_d)
    num_b, den_b = weighted(s_d * jnp.where(s_idx >= t_idx, jnp.exp(rows_d[1:2, :] - m_b), 0.0), v_d)

    num_f = num_f + jnp.where(other_is_past, num_o, 0.0)
    den_f = den_f + jnp.where(other_is_past, den_o, 0.0)
    num_b = num_b + jnp.where(other_is_past, 0.0, num_o)
    den_b = den_b + jnp.where(other_is_past, 0.0, den_o)

    h = num_f / jnp.maximum(jnp.abs(den_f), e_f) + num_b / jnp.maximum(jnp.abs(den_b), e_b)
    h = _rms(h, hn_ref[0])
    out_ref[...] = (h * jax.nn.sigmoid(o_ref[...].astype(F32))).astype(out_ref.dtype)


def mlstm_core(proj, rowp, colp, head_norm, batch, seq):
    h, dk, dv = MLSTM_HEADS, MLSTM_QK_DIM, MLSTM_V_DIM
    nq = MLSTM_BLOCKS
    assert seq % nq == 0
    tq = seq // nq
    k_col0 = h * dk // dk
    v_col0 = 2 * h * dk // dv
    o_col0 = (2 * h * dk + h * dv) // dv
    kernel = functools.partial(_mlstm_kernel, tq=tq)
    return pl.pallas_call(
        kernel,
        grid=(batch, h, nq),
        in_specs=[
            pl.BlockSpec((tq, dk), lambda b, hh, i: (b * nq + i, hh)),
            pl.BlockSpec((seq, dk), lambda b, hh, i: (b, k_col0 + hh)),
            pl.BlockSpec((seq, dv), lambda b, hh, i: (b, v_col0 + hh)),
            pl.BlockSpec((tq, dv), lambda b, hh, i: (b * nq + i, o_col0 + hh)),
            pl.BlockSpec((1, 1, nq, 8, tq), lambda b, hh, i: (b, hh, 0, 0, 0)),
            pl.BlockSpec((1, 1, tq, 8), lambda b, hh, i: (b, hh, i, 0)),
            pl.BlockSpec((1, 1, dv), lambda b, hh, i: (hh, 0, 0)),
        ],
        out_specs=pl.BlockSpec((tq, dv), lambda b, hh, i: (b * nq + i, hh)),
        out_shape=jax.ShapeDtypeStruct((batch * seq, h * dv), BF16),
        compiler_params=_params("parallel", "parallel", "arbitrary"),
    )(proj, proj, proj, proj, rowp, colp, head_norm.reshape(h, 1, dv))


def mlstm_mixer(x2d, ln, w_in, gate_bias, head_norm, w_out, batch, seq):
    h = MLSTM_HEADS
    n_main = 2 * h * MLSTM_QK_DIM + 2 * h * MLSTM_V_DIM
    w_main = w_in[:, :n_main].astype(BF16)
    w_gate = jnp.pad(w_in[:, n_main:], ((0, 0), (0, LANES - 4 * h))).astype(BF16)
    proj = norm_matmul(x2d, ln, w_main, BF16, tn=2048)
    gates = norm_matmul(x2d, ln, w_gate, F32)[:, :4 * h]
    gates_t = gates.reshape(batch, seq, 4, h).transpose(0, 2, 3, 1)
    bias = gate_bias.astype(F32)
    a, mx, en = gate_scan(gates_t[:, 0::2].reshape(batch, 2 * h, seq), gates_t[:, 1::2].reshape(batch, 2 * h, seq),
                          bias[0::2].reshape(2 * h, 1), bias[1::2].reshape(2 * h, 1))
    nk = MLSTM_BLOCKS
    tq = seq // nk
    rowp = a.reshape(batch, 2, h, nk, tq).transpose(0, 2, 3, 1, 4)
    rowp = jnp.pad(rowp, ((0, 0), (0, 0), (0, 0), (0, 6), (0, 0)))
    colp = jnp.concatenate([mx.reshape(batch, 2, h, seq), en.reshape(batch, 2, h, seq)], axis=1)
    colp = jnp.pad(colp.transpose(0, 2, 3, 1), ((0, 0), (0, 0), (0, 0), (0, 4)))
    hg = mlstm_core(proj, rowp, colp, head_norm, batch, seq)
    return matmul_residual(hg, w_out.astype(BF16), x2d)


def _qk_proj_kernel(x_ref, g_ref, w_ref, gain_ref, mult_ref, cos_ref, sin_ref, o_ref):
    xn = _rms(x_ref[...], g_ref[...]).astype(BF16)
    acc = jnp.dot(xn, w_ref[...], preferred_element_type=F32)
    hd = ATTN_HEAD_DIM
    cos = cos_ref[...]
    sin = sin_ref[...]
    for c in range(acc.shape[1] // hd):
        sl = slice(c * hd, (c + 1) * hd)
        y = _rms(acc[:, sl], gain_ref[:, sl])
        y = y * cos + pltpu.roll(y, hd // 2, axis=1) * sin
        o_ref[:, sl] = (y * mult_ref[:, sl]).astype(o_ref.dtype)


def qk_proj(x, g, w, gain, mult, cos_t, sin_t, seq, *, tm=512):
    m, k = x.shape
    n = w.shape[1]
    nseq = seq // tm
    return pl.pallas_call(
        _qk_proj_kernel,
        grid=(m // tm,),
        in_specs=[
            pl.BlockSpec((tm, k), lambda i: (i, 0)),
            pl.BlockSpec((1, k), lambda i: (0, 0)),
            pl.BlockSpec((k, n), lambda i: (0, 0)),
            pl.BlockSpec((1, n), lambda i: (0, 0)),
            pl.BlockSpec((1, n), lambda i: (0, 0)),
            pl.BlockSpec((tm, ATTN_HEAD_DIM), lambda i: (i % nseq, 0)),
            pl.BlockSpec((tm, ATTN_HEAD_DIM), lambda i: (i % nseq, 0)),
        ],
        out_specs=pl.BlockSpec((tm, n), lambda i: (i, 0)),
        out_shape=jax.ShapeDtypeStruct((m, n), BF16),
        compiler_params=_params("parallel"),
    )(x, g.reshape(1, k), w, gain, mult, cos_t, sin_t)


def _attn_kernel(q_ref, k_ref, v_ref, o_ref):
    hd = ATTN_HEAD_DIM
    k = k_ref[...]
    v = v_ref[...]
    for g in range(ATTN_GROUPS):
        sl = slice(g * hd, (g + 1) * hd)
        s = lax.dot_general(q_ref[:, sl], k, _NT, preferred_element_type=F32)
        p = jnp.exp(s - jnp.max(s, axis=-1, keepdims=True))
        l = jnp.sum(p, axis=-1, keepdims=True)
        o = jnp.dot(p.astype(BF16), v, preferred_element_type=F32)
        o_ref[:, sl] = (o / l).astype(o_ref.dtype)


def attn_core(qk, v, batch, seq, *, tq=512):
    hd, g = ATTN_HEAD_DIM, ATTN_GROUPS
    nq = seq // tq
    return pl.pallas_call(
        _attn_kernel,
        grid=(batch, ATTN_KV_HEADS, nq),
        in_specs=[
            pl.BlockSpec((tq, g * hd), lambda b, kv, i: (b * nq + i, kv)),
            pl.BlockSpec((seq, hd), lambda b, kv, i: (b, ATTN_Q_HEADS + kv)),
            pl.BlockSpec((seq, hd), lambda b, kv, i: (b, kv)),
        ],
        out_specs=pl.BlockSpec((tq, g * hd), lambda b, kv, i: (b * nq + i, kv)),
        out_shape=jax.ShapeDtypeStruct((batch * seq, ATTN_Q_HEADS * hd), BF16),
        compiler_params=_params("parallel", "parallel", "arbitrary"),
    )(qk, qk, v)


def _rope_tables(seq):
    rows = seq // GRID_W
    row = jnp.repeat(jnp.arange(rows, dtype=F32), GRID_W)
    col = jnp.tile(jnp.arange(GRID_W, dtype=F32), rows)
    n_freq = ATTN_HEAD_DIM // 4
    freqs = ROPE_THETA ** (-jnp.arange(n_freq, dtype=F32) / n_freq)
    ang = jnp.concatenate([row[:, None] * freqs, col[:, None] * freqs], axis=-1)
    cos, sin = jnp.cos(ang), jnp.sin(ang)
    return jnp.concatenate([cos, cos], axis=-1), jnp.concatenate([-sin, sin], axis=-1)


def attn_mixer(x2d, ln, w_in, q_norm, k_norm, w_out, batch, seq):
    hd = ATTN_HEAD_DIM
    n_qk_heads = ATTN_Q_HEADS + ATTN_KV_HEADS
    n_qk = n_qk_heads * hd
    perm = jnp.concatenate([jnp.arange(0, hd, 2), jnp.arange(1, hd, 2)])
    d_model = w_in.shape[0]
    w_qk = w_in[:, :n_qk].reshape(d_model, n_qk_heads, hd)[:, :, perm].reshape(d_model, n_qk).astype(BF16)
    w_v = w_in[:, n_qk:].astype(BF16)
    gain = jnp.concatenate([jnp.tile(q_norm[perm], ATTN_Q_HEADS), jnp.tile(k_norm[perm], ATTN_KV_HEADS)])
    mult = jnp.concatenate([jnp.full((ATTN_Q_HEADS * hd,), hd ** -0.5, F32),
                            jnp.ones((ATTN_KV_HEADS * hd,), F32)])
    cos_t, sin_t = _rope_tables(seq)
    qk = qk_proj(x2d, ln, w_qk, gain.reshape(1, n_qk).astype(F32), mult.reshape(1, n_qk), cos_t, sin_t, seq)
    v = norm_matmul(x2d, ln, w_v, BF16)
    o = attn_core(qk, v, batch, seq)
    return matmul_residual(o, w_out.astype(BF16), x2d)


_PEER_CAND = [(i, j) for i in range(PEER_TOPK) for j in range(min(PEER_TOPK, (PEER_TOPK + 1) // (i + 1)))]
_PEER_CAND_ROWS = -(-len(_PEER_CAND) // 8) * 8


def _top_values(x, n):
    vals = []
    for _ in range(n):
        m = jnp.max(x, axis=0, keepdims=True)
        vals.append(m)
        x = jnp.where(x == m, -jnp.inf, x)
    return vals


def _sorting_network(n):
    pairs = []
    p = 1
    while p < n:
        k = p
        while k >= 1:
            for j in range(k % p, n - k, 2 * k):
                for i in range(min(k, n - j - k)):
                    if (i + j) // (2 * p) == (i + j + k) // (2 * p):
                        pairs.append((i + j, i + j + k))
            k //= 2
        p *= 2
    return pairs


def _top_values_of_keys(x, n):
    slabs = [x[r:r + 8, :] for r in range(0, x.shape[0], 8)]
    for i, j in _sorting_network(len(slabs)):
        slabs[i], slabs[j] = jnp.maximum(slabs[i], slabs[j]), jnp.minimum(slabs[i], slabs[j])
    vals = []
    for k in range(n):
        m = jnp.max(slabs[0], axis=0, keepdims=True)
        vals.append(m)
        hit = slabs[0] == m
        for r in range(min(n - 1 - k, len(slabs))):
            below = slabs[r + 1] if r + 1 < len(slabs) else -jnp.inf
            slabs[r] = jnp.where(hit, below, slabs[r])
    return vals


def _peer_route_kernel(q_ref, k1_ref, k2_ref, thr_ref, c_ref, e2_ref, s2_ref, cand_ref):
    nk, half, topk = PEER_N_KEYS, PEER_HALF, PEER_TOPK
    k1 = k1_ref[...]
    k2 = k2_ref[...]
    tm = q_ref.shape[0]
    cand_ref[...] = jnp.full(cand_ref.shape, -jnp.inf, F32)
    for h in range(PEER_HEADS):
        qa = q_ref[:, (2 * h) * half:(2 * h + 1) * half]
        qb = q_ref[:, (2 * h + 1) * half:(2 * h + 2) * half]
        s1 = lax.dot_general(k1, qa, _NT, preferred_element_type=F32)
        s2 = lax.dot_general(k2, qb, _NT, preferred_element_type=F32)
        t1 = _top_values_of_keys(s1, topk + 1)
        t2 = _top_values_of_keys(s2, topk + 1)
        for r, (i, j) in enumerate(_PEER_CAND):
            cand_ref[r:r + 1, :] = t1[i] + t2[j]
        c = _top_values(cand_ref[...], topk + 1)
        c_next = jnp.maximum(c[topk], jnp.maximum(t1[topk] + t2[0], t1[0] + t2[topk]))
        thr = 0.5 * (c[topk - 1] + c_next)
        z = jnp.zeros((1, tm), F32)
        for kk in range(topk):
            z = z + jnp.exp(c[kk] - c[0])
        rows = slice(h * nk, (h + 1) * nk)
        e2 = jnp.exp(s2 - t2[0])
        for lc in range(tm // LANES):
            cols = slice(lc * LANES, (lc + 1) * LANES)
            e2_ref[lc, rows, :] = e2[:, cols]
            s2_ref[lc, rows, :] = s2[:, cols]
        n_sub = thr_ref.shape[1] // PEER_HEADS
        thr_a = thr - s1
        c_a = jnp.exp(s1 - t1[0]) / z
        for blk in range(nk // n_sub):
            thr_ref[blk, h * n_sub:(h + 1) * n_sub, :] = thr_a[blk * n_sub:(blk + 1) * n_sub, :]
            c_ref[blk, h * n_sub:(h + 1) * n_sub, :] = c_a[blk * n_sub:(blk + 1) * n_sub, :]


def peer_route(q, k1, k2, n_sub, *, tm=512):
    t, d = q.shape
    rows = PEER_HEADS * PEER_N_KEYS
    blocks = PEER_N_KEYS // n_sub
    a_spec = pl.BlockSpec((blocks, PEER_HEADS * n_sub, tm), lambda i: (0, 0, i))
    a_shape = jax.ShapeDtypeStruct((blocks, PEER_HEADS * n_sub, t), F32)
    b_spec = pl.BlockSpec((tm // LANES, rows, LANES), lambda i: (i, 0, 0))
    b_shape = jax.ShapeDtypeStruct((t // LANES, rows, LANES), F32)
    return pl.pallas_call(
        _peer_route_kernel,
        grid=(t // tm,),
        in_specs=[
            pl.BlockSpec((tm, d), lambda i: (i, 0)),
            pl.BlockSpec((PEER_N_KEYS, PEER_HALF), lambda i: (0, 0)),
            pl.BlockSpec((PEER_N_KEYS, PEER_HALF), lambda i: (0, 0)),
        ],
        out_specs=[a_spec, a_spec, b_spec, b_spec],
        out_shape=[a_shape, a_shape, b_shape, b_shape],
        scratch_shapes=[pltpu.VMEM((_PEER_CAND_ROWS, tm), F32)],
        compiler_params=_params("parallel"),
    )(q, k1, k2)


def _gelu_tanh(x):
    return 0.5 * x * (1.0 + jnp.tanh(math.sqrt(2.0 / math.pi) * (x + 0.044715 * (x * x * x))))


def _peer_expert_kernel(xt_ref, u_ref, vt_ref, thr_ref, c_ref, e2_ref, s2_ref, res_ref, zero_ref, o_ref,
                        acc_ref, w0_ref, w1_ref, g_ref, *, n_blocks):
    e = pl.program_id(1)
    nk = PEER_N_KEYS
    eb, d = u_ref.shape
    tt = xt_ref.shape[1]
    n_sub = eb // nk
    lane_tiles = tt // LANES
    n_stages = eb // PEER_ACT_CHUNK
    keys_per_stage = PEER_ACT_CHUNK // nk
    out_per_stage = d // PEER_OUT_CHUNK // n_stages

    def gate_tiles(a):
        zeros = []
        sub = PEER_GATE_ROWS
        for lc in range(lane_tiles):
            cols = slice(lc * LANES, (lc + 1) * LANES)
            folded = None
            for b0 in range(0, nk, sub):
                gate = jnp.zeros((sub, LANES), F32)
                for h in range(PEER_HEADS):
                    r = h * n_sub + a
                    rows = slice(h * nk + b0, h * nk + b0 + sub)
                    gate = gate + jnp.where(s2_ref[lc, rows, :] >= thr_ref[0, r:r + 1, cols],
                                            e2_ref[lc, rows, :] * c_ref[0, r:r + 1, cols], 0.0)
                g_ref[a * nk + b0:a * nk + b0 + sub, cols] = gate
                for r in range(0, sub, 8):
                    folded = gate[r:r + 8, :] if folded is None else folded + gate[r:r + 8, :]
            zeros.append(pltpu.bitcast(folded, jnp.uint32) & zero_ref[...])
        return zeros

    def paced(lhs, zeros):
        if not zeros:
            return lhs
        z = functools.reduce(jnp.bitwise_or, zeros)
        words = pltpu.bitcast(lhs, jnp.uint32)
        words = words | jnp.tile(z, (words.shape[0] // z.shape[0], words.shape[1] // z.shape[1]))
        return pltpu.bitcast(words, lhs.dtype)

    def run(w_old_ref, w_new_ref):
        for s in range(n_stages):
            zeros = []
            if w_new_ref is not None:
                for a in range(s * keys_per_stage, (s + 1) * keys_per_stage):
                    zeros += gate_tiles(a)
            if w_old_ref is not None:
                for k in range(s * out_per_stage, (s + 1) * out_per_stage):
                    rows = slice(k * PEER_OUT_CHUNK, (k + 1) * PEER_OUT_CHUNK)
                    share = len(zeros) // out_per_stage
                    lhs = paced(vt_ref[rows, :], zeros[(k % out_per_stage) * share:(k % out_per_stage + 1) * share])
                    acc_ref[rows, :] += jnp.dot(lhs, w_old_ref[...], preferred_element_type=F32)
            if w_new_ref is not None:
                rows = slice(s * PEER_ACT_CHUNK, (s + 1) * PEER_ACT_CHUNK)
                act = _gelu_tanh(jnp.dot(paced(u_ref[rows, :], zeros), xt_ref[...], preferred_element_type=F32))
                w_new_ref[rows, :] = (g_ref[rows, :] * act).astype(BF16)

    @pl.when(e == 0)
    def _():
        acc_ref[...] = jnp.zeros_like(acc_ref)
        run(None, w0_ref)

    is_mid = jnp.logical_and(e > 0, e < n_blocks)

    @pl.when(jnp.logical_and(is_mid, e % 2 == 1))
    def _():
        run(w0_ref, w1_ref)

    @pl.when(jnp.logical_and(is_mid, e % 2 == 0))
    def _():
        run(w1_ref, w0_ref)

    @pl.when(e == n_blocks)
    def _():
        run(w1_ref if n_blocks % 2 == 0 else w0_ref, None)
        o_ref[...] = res_ref[...] + acc_ref[...].T


def peer_experts(xt, u, vt, thr, c, e2, s2, res, *, tt=512):
    d, t = xt.shape
    n_exp = u.shape[0]
    eb = PEER_EXPERT_BLOCK
    n_blocks = n_exp // eb
    rows = PEER_HEADS * PEER_N_KEYS
    once = pl.Buffered(1)
    side_a = pl.BlockSpec((1, thr.shape[1], tt), lambda i, e: (jnp.minimum(e, n_blocks - 1), 0, i))
    side_b = pl.BlockSpec((tt // LANES, rows, LANES), lambda i, e: (i, 0, 0), pipeline_mode=once)
    kernel = functools.partial(_peer_expert_kernel, n_blocks=n_blocks)
    return pl.pallas_call(
        kernel,
        grid=(t // tt, n_blocks + 1),
        in_specs=[
            pl.BlockSpec((d, tt), lambda i, e: (0, i), pipeline_mode=once),
            pl.BlockSpec((eb, d), lambda i, e: (jnp.minimum(e, n_blocks - 1), 0)),
            pl.BlockSpec((d, eb), lambda i, e: (0, jnp.maximum(e - 1, 0))),
            side_a, side_a, side_b, side_b,
            pl.BlockSpec((tt, d), lambda i, e: (i, 0), pipeline_mode=once),
            pl.BlockSpec((8, LANES), lambda i, e: (0, 0)),
        ],
        out_specs=pl.BlockSpec((tt, d), lambda i, e: (i, 0)),
        out_shape=jax.ShapeDtypeStruct((t, d), F32),
        scratch_shapes=[pltpu.VMEM((d, tt), F32), pltpu.VMEM((eb, tt), BF16), pltpu.VMEM((eb, tt), BF16),
                        pltpu.VMEM((eb, tt), F32)],
        compiler_params=_params("parallel", "arbitrary"),
    )(xt, u, vt, thr, c, e2, s2, res, jnp.zeros((8, LANES), jnp.uint32))


def _transpose_cast_kernel(x_ref, o_ref):
    o_ref[...] = x_ref[0].T.astype(o_ref.dtype)


def transpose_cast(x, layer, dtype, *, tr=1024):
    _, r, c = x.shape
    return pl.pallas_call(
        _transpose_cast_kernel,
        grid=(r // tr,),
        in_specs=[pl.BlockSpec((1, tr, c), lambda i: (layer, i, 0))],
        out_specs=pl.BlockSpec((c, tr), lambda i: (0, i)),
        out_shape=jax.ShapeDtypeStruct((c, r), dtype),
        compiler_params=_params("parallel"),
    )(x)


def _cast_kernel(x_ref, o_ref):
    o_ref[...] = x_ref[0].astype(o_ref.dtype)


def layer_cast(x, layer, dtype, *, tr=1024):
    _, r, c = x.shape
    return pl.pallas_call(
        _cast_kernel,
        grid=(r // tr,),
        in_specs=[pl.BlockSpec((1, tr, c), lambda i: (layer, i, 0))],
        out_specs=pl.BlockSpec((tr, c), lambda i: (i, 0)),
        out_shape=jax.ShapeDtypeStruct((r, c), dtype),
        compiler_params=_params("parallel"),
    )(x)


def peer_mixer(x2d, ln, wq, k1, k2, u_all, v_all, layer):
    q, xt = norm_matmul(x2d, ln, wq.astype(BF16), BF16, tm=512, tn=wq.shape[1], with_transposed_norm=True)
    thr, c, e2, s2 = peer_route(q, k1.astype(BF16), k2.astype(BF16), PEER_EXPERT_BLOCK // PEER_N_KEYS)
    return peer_experts(xt, layer_cast(u_all, layer, BF16), transpose_cast(v_all, layer, BF16), thr, c, e2, s2, x2d)


def kernel(x, ln_mix, ln_ffn, mlstm_w_in, mlstm_gate_bias, mlstm_head_norm, mlstm_w_out,
           attn_w_in, attn_q_norm, attn_k_norm, attn_w_out,
           peer_wq, peer_k1, peer_k2, peer_u, peer_v):
    batch, seq, d_model = x.shape
    depth = ln_mix.shape[0]
    x2d = x.reshape(batch * seq, d_model)
    for layer in range(depth):
        j = layer // 2
        if layer % 2 == 0:
            x2d = mlstm_mixer(x2d, ln_mix[layer], mlstm_w_in[j], mlstm_gate_bias[j], mlstm_head_norm[j],
                              mlstm_w_out[j], batch, seq)
        else:
            x2d = attn_mixer(x2d, ln_mix[layer], attn_w_in[j], attn_q_norm[j], attn_k_norm[j],
                             attn_w_out[j], batch, seq)
        x2d = peer_mixer(x2d, ln_ffn[layer], peer_wq[layer], peer_k1[layer], peer_k2[layer],
                         peer_u, peer_v, layer)
    return x2d.reshape(batch, seq, d_model)
```

```python
import functools
import math

import jax
import jax.numpy as jnp
from jax import lax
from jax.experimental import pallas as pl
from jax.experimental.pallas import tpu as pltpu

NORM_EPS = 1e-6

MLSTM_HEADS = 4
MLSTM_QK_DIM = 256
MLSTM_V_DIM = 512
MLSTM_BLOCKS = 2

ATTN_HEAD_DIM = 128
ATTN_Q_HEADS = 16
ATTN_KV_HEADS = 4
ATTN_GROUPS = ATTN_Q_HEADS // ATTN_KV_HEADS
GRID_W = 64
ROPE_THETA = 10000.0

PEER_HEADS = 8
PEER_N_KEYS = 128
PEER_TOPK = 16
PEER_HALF = 128
PEER_EXPERT_BLOCK = 1024
PEER_OUT_CHUNK = 256
PEER_ACT_CHUNK = 256
PEER_GATE_ROWS = 32

LANES = 128
VMEM_LIMIT_BYTES = 56 * 1024 * 1024

BF16 = jnp.bfloat16
F32 = jnp.float32

_NT = (((1,), (1,)), ((), ()))


def _params(*sem):
    return pltpu.CompilerParams(dimension_semantics=sem, vmem_limit_bytes=VMEM_LIMIT_BYTES)


def _rms(x, g):
    return x * lax.rsqrt(jnp.mean(x * x, axis=-1, keepdims=True) + NORM_EPS) * g


def _norm_matmul_kernel(x_ref, g_ref, w_ref, o_ref, xn_ref):
    @pl.when(pl.program_id(1) == 0)
    def _():
        xn_ref[...] = _rms(x_ref[...], g_ref[...]).astype(BF16)

    o_ref[...] = jnp.dot(xn_ref[...], w_ref[...], preferred_element_type=F32).astype(o_ref.dtype)


def _norm_matmul_t_kernel(x_ref, g_ref, w_ref, o_ref, xt_ref):
    y = _rms(x_ref[...], g_ref[...])
    xt_ref[...] = y.T.astype(BF16)
    o_ref[...] = jnp.dot(y.astype(BF16), w_ref[...], preferred_element_type=F32).astype(o_ref.dtype)


def norm_matmul(x, g, w, out_dtype, *, tm=1024, tn=1024, with_transposed_norm=False):
    m, k = x.shape
    n = w.shape[1]
    tn = min(tn, n)
    grid = (m // tm, n // tn)
    in_specs = [
        pl.BlockSpec((tm, k), lambda i, j: (i, 0)),
        pl.BlockSpec((1, k), lambda i, j: (0, 0)),
        pl.BlockSpec((k, tn), lambda i, j: (0, j)),
    ]
    o_spec = pl.BlockSpec((tm, tn), lambda i, j: (i, j))
    scratch = [pltpu.VMEM((tm, k), BF16)]
    if with_transposed_norm:
        assert grid[1] == 1
        return pl.pallas_call(
            _norm_matmul_t_kernel,
            grid=grid, in_specs=in_specs,
            out_specs=[o_spec, pl.BlockSpec((k, tm), lambda i, j: (0, i))],
            out_shape=[jax.ShapeDtypeStruct((m, n), out_dtype), jax.ShapeDtypeStruct((k, m), BF16)],
            compiler_params=_params("parallel", "arbitrary"),
        )(x, g.reshape(1, k), w)
    return pl.pallas_call(
        _norm_matmul_kernel,
        grid=grid, in_specs=in_specs, out_specs=o_spec,
        out_shape=jax.ShapeDtypeStruct((m, n), out_dtype),
        scratch_shapes=scratch,
        compiler_params=_params("parallel", "arbitrary"),
    )(x, g.reshape(1, k), w)


def _matmul_residual_kernel(x_ref, w_ref, r_ref, o_ref):
    o_ref[...] = r_ref[...] + jnp.dot(x_ref[...], w_ref[...], preferred_element_type=F32)


def matmul_residual(x, w, res, *, tm=512, tn=2048):
    m, k = x.shape
    n = w.shape[1]
    return pl.pallas_call(
        _matmul_residual_kernel,
        grid=(m // tm, n // tn),
        in_specs=[
            pl.BlockSpec((tm, k), lambda i, j: (i, 0)),
            pl.BlockSpec((k, tn), lambda i, j: (0, j)),
            pl.BlockSpec((tm, tn), lambda i, j: (i, j)),
        ],
        out_specs=pl.BlockSpec((tm, tn), lambda i, j: (i, j)),
        out_shape=jax.ShapeDtypeStruct((m, n), F32),
        compiler_params=_params("parallel", "arbitrary"),
    )(x, w, res)


def _scan_lanes(x, op, fill, is_suffix):
    s = x.shape[-1]
    lane = lax.broadcasted_iota(jnp.int32, x.shape, 1)
    d = 1
    while d < s:
        from_left = jnp.where(lane >= d, pltpu.roll(x, d, axis=1), fill)
        from_right = jnp.where(lane < s - d, pltpu.roll(x, s - d, axis=1), fill)
        x = op(x, jnp.where(is_suffix, from_right, from_left))
        d *= 2
    return x


def _gate_scan_kernel(ig_ref, fg_ref, bi_ref, bf_ref, a_ref, m_ref, e_ref):
    ig = ig_ref[0] + bi_ref[...]
    fg = fg_ref[0] + bf_ref[...]
    is_suffix = lax.broadcasted_iota(jnp.int32, ig.shape, 0) >= MLSTM_HEADS
    logsig = jnp.minimum(fg, 0.0) - jnp.log1p(jnp.exp(-jnp.abs(fg)))
    fcum = _scan_lanes(logsig, jnp.add, 0.0, is_suffix)
    a = ig - fcum
    mx = jnp.maximum(_scan_lanes(a, jnp.maximum, -jnp.inf, is_suffix), 0.0)
    a_ref[0] = a
    m_ref[0] = mx
    e_ref[0] = jnp.exp(-(fcum + mx))


def gate_scan(ig, fg, bi, bf):
    b, r, s = ig.shape
    g_spec = pl.BlockSpec((1, r, s), lambda i: (i, 0, 0))
    b_spec = pl.BlockSpec((r, 1), lambda i: (0, 0))
    o_shape = jax.ShapeDtypeStruct((b, r, s), F32)
    return pl.pallas_call(
        _gate_scan_kernel,
        grid=(b,),
        in_specs=[g_spec, g_spec, b_spec, b_spec],
        out_specs=[g_spec] * 3,
        out_shape=[o_shape] * 3,
        compiler_params=_params("parallel"),
    )(ig, fg, bi, bf)


def _mlstm_kernel(q_ref, k_ref, v_ref, o_ref, row_ref, col_ref, hn_ref, out_ref, *, tq):
    i = pl.program_id(2)
    j = 1 - i
    other_is_past = i > j
    q = q_ref[...]
    col = col_ref[0, 0]
    m_f, m_b, e_f, e_b = col[:, 0:1], col[:, 1:2], col[:, 2:3], col[:, 3:4]
    scale = MLSTM_QK_DIM ** -0.5

    def scores(blk):
        start = pl.multiple_of(blk * tq, tq)
        kb = k_ref[pl.ds(start, tq), :]
        vb = v_ref[pl.ds(start, tq), :]
        return lax.dot_general(q, kb, _NT, preferred_element_type=F32) * scale, vb

    def weighted(p, vb):
        return jnp.dot(p.astype(BF16), vb, preferred_element_type=F32), jnp.sum(p, axis=-1, keepdims=True)

    s_o, v_o = scores(j)
    rows_o = row_ref[0, 0, j]
    a_o = jnp.where(other_is_past, rows_o[0:1, :], rows_o[1:2, :])
    m_o = jnp.where(other_is_past, m_f, m_b)
    num_o, den_o = weighted(s_o * jnp.exp(a_o - m_o), v_o)

    s_d, v_d = scores(i)
    rows_d = row_ref[0, 0, i]
    t_idx = lax.broadcasted_iota(jnp.int32, (tq, tq), 0)
    s_idx = lax.broadcasted_iota(jnp.int32, (tq, tq), 1)
    num_f, den_f = weighted(s_d * jnp.where(s_idx <= t_idx, jnp.exp(rows_d[0:1, :] - m_f), 0.0), v_d)
    num_b, den_b = weighted(s_d * jnp.where(s_idx >= t_idx, jnp.exp(rows_d[1:2, :] - m_b), 0.0), v_d)

    num_f = num_f + jnp.where(other_is_past, num_o, 0.0)
    den_f = den_f + jnp.where(other_is_past, den_o, 0.0)
    num_b = num_b + jnp.where(other_is_past, 0.0, num_o)
    den_b = den_b + jnp.where(other_is_past, 0.0, den_o)

    h = num_f / jnp.maximum(jnp.abs(den_f), e_f) + num_b / jnp.maximum(jnp.abs(den_b), e_b)
    h = _rms(h, hn_ref[0])
    out_ref[...] = (h * jax.nn.sigmoid(o_ref[...].astype(F32))).astype(out_ref.dtype)


def mlstm_core(proj, rowp, colp, head_norm, batch, seq):
    h, dk, dv = MLSTM_HEADS, MLSTM_QK_DIM, MLSTM_V_DIM
    nq = MLSTM_BLOCKS
    assert seq % nq == 0
    tq = seq // nq
    k_col0 = h * dk // dk
    v_col0 = 2 * h * dk // dv
    o_col0 = (2 * h * dk + h * dv) // dv
    kernel = functools.partial(_mlstm_kernel, tq=tq)
    return pl.pallas_call(
        kernel,
        grid=(batch, h, nq),
        in_specs=[
            pl.BlockSpec((tq, dk), lambda b, hh, i: (b * nq + i, hh)),
            pl.BlockSpec((seq, dk), lambda b, hh, i: (b, k_col0 + hh)),
            pl.BlockSpec((seq, dv), lambda b, hh, i: (b, v_col0 + hh)),
            pl.BlockSpec((tq, dv), lambda b, hh, i: (b * nq + i, o_col0 + hh)),
            pl.BlockSpec((1, 1, nq, 8, tq), lambda b, hh, i: (b, hh, 0, 0, 0)),
            pl.BlockSpec((1, 1, tq, 8), lambda b, hh, i: (b, hh, i, 0)),
            pl.BlockSpec((1, 1, dv), lambda b, hh, i: (hh, 0, 0)),
        ],
        out_specs=pl.BlockSpec((tq, dv), lambda b, hh, i: (b * nq + i, hh)),
        out_shape=jax.ShapeDtypeStruct((batch * seq, h * dv), BF16),
        compiler_params=_params("parallel", "parallel", "arbitrary"),
    )(proj, proj, proj, proj, rowp, colp, head_norm.reshape(h, 1, dv))


def mlstm_mixer(x2d, ln, w_in, gate_bias, head_norm, w_out, batch, seq):
    h = MLSTM_HEADS
    n_main = 2 * h * MLSTM_QK_DIM + 2 * h * MLSTM_V_DIM
    w_main = w_in[:, :n_main].astype(BF16)
    w_gate = jnp.pad(w_in[:, n_main:], ((0, 0), (0, LANES - 4 * h))).astype(BF16)
    proj = norm_matmul(x2d, ln, w_main, BF16, tn=2048)
    gates = norm_matmul(x2d, ln, w_gate, F32)[:, :4 * h]
    gates_t = gates.reshape(batch, seq, 4, h).transpose(0, 2, 3, 1)
    bias = gate_bias.astype(F32)
    a, mx, en = gate_scan(gates_t[:, 0::2].reshape(batch, 2 * h, seq), gates_t[:, 1::2].reshape(batch, 2 * h, seq),
                          bias[0::2].reshape(2 * h, 1), bias[1::2].reshape(2 * h, 1))
    nk = MLSTM_BLOCKS
    tq = seq // nk
    rowp = a.reshape(batch, 2, h, nk, tq).transpose(0, 2, 3, 1, 4)
    rowp = jnp.pad(rowp, ((0, 0), (0, 0), (0, 0), (0, 6), (0, 0)))
    colp = jnp.concatenate([mx.reshape(batch, 2, h, seq), en.reshape(batch, 2, h, seq)], axis=1)
    colp = jnp.pad(colp.transpose(0, 2, 3, 1), ((0, 0), (0, 0), (0, 0), (0, 4)))
    hg = mlstm_core(proj, rowp, colp, head_norm, batch, seq)
    return matmul_residual(hg, w_out.astype(BF16), x2d)


def _qk_proj_kernel(x_ref, g_ref, w_ref, gain_ref, mult_ref, cos_ref, sin_ref, o_ref):
    xn = _rms(x_ref[...], g_ref[...]).astype(BF16)
    acc = jnp.dot(xn, w_ref[...], preferred_element_type=F32)
    hd = ATTN_HEAD_DIM
    cos = cos_ref[...]
    sin = sin_ref[...]
    for c in range(acc.shape[1] // hd):
        sl = slice(c * hd, (c + 1) * hd)
        y = _rms(acc[:, sl], gain_ref[:, sl])
        y = y * cos + pltpu.roll(y, hd // 2, axis=1) * sin
        o_ref[:, sl] = (y * mult_ref[:, sl]).astype(o_ref.dtype)


def qk_proj(x, g, w, gain, mult, cos_t, sin_t, seq, *, tm=512):
    m, k = x.shape
    n = w.shape[1]
    nseq = seq // tm
    return pl.pallas_call(
        _qk_proj_kernel,
        grid=(m // tm,),
        in_specs=[
            pl.BlockSpec((tm, k), lambda i: (i, 0)),
            pl.BlockSpec((1, k), lambda i: (0, 0)),
            pl.BlockSpec((k, n), lambda i: (0, 0)),
            pl.BlockSpec((1, n), lambda i: (0, 0)),
            pl.BlockSpec((1, n), lambda i: (0, 0)),
            pl.BlockSpec((tm, ATTN_HEAD_DIM), lambda i: (i % nseq, 0)),
            pl.BlockSpec((tm, ATTN_HEAD_DIM), lambda i: (i % nseq, 0)),
        ],
        out_specs=pl.BlockSpec((tm, n), lambda i: (i, 0)),
        out_shape=jax.ShapeDtypeStruct((m, n), BF16),
        compiler_params=_params("parallel"),
    )(x, g.reshape(1, k), w, gain, mult, cos_t, sin_t)


def _attn_kernel(q_ref, k_ref, v_ref, o_ref):
    hd = ATTN_HEAD_DIM
    k = k_ref[...]
    v = v_ref[...]
    for g in range(ATTN_GROUPS):
        sl = slice(g * hd, (g + 1) * hd)
        s = lax.dot_general(q_ref[:, sl], k, _NT, preferred_element_type=F32)
        p = jnp.exp(s - jnp.max(s, axis=-1, keepdims=True))
        l = jnp.sum(p, axis=-1, keepdims=True)
        o = jnp.dot(p.astype(BF16), v, preferred_element_type=F32)
        o_ref[:, sl] = (o / l).astype(o_ref.dtype)


def attn_core(qk, v, batch, seq, *, tq=512):
    hd, g = ATTN_HEAD_DIM, ATTN_GROUPS
    nq = seq // tq
    return pl.pallas_call(
        _attn_kernel,
        grid=(batch, ATTN_KV_HEADS, nq),
        in_specs=[
            pl.BlockSpec((tq, g * hd), lambda b, kv, i: (b * nq + i, kv)),
            pl.BlockSpec((seq, hd), lambda b, kv, i: (b, ATTN_Q_HEADS + kv)),
            pl.BlockSpec((seq, hd), lambda b, kv, i: (b, kv)),
        ],
        out_specs=pl.BlockSpec((tq, g * hd), lambda b, kv, i: (b * nq + i, kv)),
        out_shape=jax.ShapeDtypeStruct((batch * seq, ATTN_Q_HEADS * hd), BF16),
        compiler_params=_params("parallel", "parallel", "arbitrary"),
    )(qk, qk, v)


def _rope_tables(seq):
    rows = seq // GRID_W
    row = jnp.repeat(jnp.arange(rows, dtype=F32), GRID_W)
    col = jnp.tile(jnp.arange(GRID_W, dtype=F32), rows)
    n_freq = ATTN_HEAD_DIM // 4
    freqs = ROPE_THETA ** (-jnp.arange(n_freq, dtype=F32) / n_freq)
    ang = jnp.concatenate([row[:, None] * freqs, col[:, None] * freqs], axis=-1)
    cos, sin = jnp.cos(ang), jnp.sin(ang)
    return jnp.concatenate([cos, cos], axis=-1), jnp.concatenate([-sin, sin], axis=-1)


def attn_mixer(x2d, ln, w_in, q_norm, k_norm, w_out, batch, seq):
    hd = ATTN_HEAD_DIM
    n_qk_heads = ATTN_Q_HEADS + ATTN_KV_HEADS
    n_qk = n_qk_heads * hd
    perm = jnp.concatenate([jnp.arange(0, hd, 2), jnp.arange(1, hd, 2)])
    d_model = w_in.shape[0]
    w_qk = w_in[:, :n_qk].reshape(d_model, n_qk_heads, hd)[:, :, perm].reshape(d_model, n_qk).astype(BF16)
    w_v = w_in[:, n_qk:].astype(BF16)
    gain = jnp.concatenate([jnp.tile(q_norm[perm], ATTN_Q_HEADS), jnp.tile(k_norm[perm], ATTN_KV_HEADS)])
    mult = jnp.concatenate([jnp.full((ATTN_Q_HEADS * hd,), hd ** -0.5, F32),
                            jnp.ones((ATTN_KV_HEADS * hd,), F32)])
    cos_t, sin_t = _rope_tables(seq)
    qk = qk_proj(x2d, ln, w_qk, gain.reshape(1, n_qk).astype(F32), mult.reshape(1, n_qk), cos_t, sin_t, seq)
    v = norm_matmul(x2d, ln, w_v, BF16)
    o = attn_core(qk, v, batch, seq)
    return matmul_residual(o, w_out.astype(BF16), x2d)


_PEER_CAND = [(i, j) for i in range(PEER_TOPK) for j in range(min(PEER_TOPK, (PEER_TOPK + 1) // (i + 1)))]
_PEER_CAND_ROWS = -(-len(_PEER_CAND) // 8) * 8


def _top_values(x, n):
    vals = []
    for _ in range(n):
        m = jnp.max(x, axis=0, keepdims=True)
        vals.append(m)
        x = jnp.where(x == m, -jnp.inf, x)
    return vals


def _sorting_network(n):
    pairs = []
    p = 1
    while p < n:
        k = p
        while k >= 1:
            for j in range(k % p, n - k, 2 * k):
                for i in range(min(k, n - j - k)):
                    if (i + j) // (2 * p) == (i + j + k) // (2 * p):
                        pairs.append((i + j, i + j + k))
            k //= 2
        p *= 2
    return pairs


def _top_values_of_keys(x, n):
    slabs = [x[r:r + 8, :] for r in range(0, x.shape[0], 8)]
    for i, j in _sorting_network(len(slabs)):
        slabs[i], slabs[j] = jnp.maximum(slabs[i], slabs[j]), jnp.minimum(slabs[i], slabs[j])
    vals = []
    for k in range(n):
        m = jnp.max(slabs[0], axis=0, keepdims=True)
        vals.append(m)
        hit = slabs[0] == m
        for r in range(min(n - 1 - k, len(slabs))):
            below = slabs[r + 1] if r + 1 < len(slabs) else -jnp.inf
            slabs[r] = jnp.where(hit, below, slabs[r])
    return vals


def _peer_route_kernel(q_ref, k1_ref, k2_ref, thr_ref, c_ref, e2_ref, s2_ref, cand_ref):
    nk, half, topk = PEER_N_KEYS, PEER_HALF, PEER_TOPK
    k1 = k1_ref[...]
    k2 = k2_ref[...]
    tm = q_ref.shape[0]
    cand_ref[...] = jnp.full(cand_ref.shape, -jnp.inf, F32)
    for h in range(PEER_HEADS):
        qa = q_ref[:, (2 * h) * half:(2 * h + 1) * half]
        qb = q_ref[:, (2 * h + 1) * half:(2 * h + 2) * half]
        s1 = lax.dot_general(k1, qa, _NT, preferred_element_type=F32)
        s2 = lax.dot_general(k2, qb, _NT, preferred_element_type=F32)
        t1 = _top_values_of_keys(s1, topk + 1)
        t2 = _top_values_of_keys(s2, topk + 1)
        for r, (i, j) in enumerate(_PEER_CAND):
            cand_ref[r:r + 1, :] = t1[i] + t2[j]
        c = _top_values(cand_ref[...], topk + 1)
        c_next = jnp.maximum(c[topk], jnp.maximum(t1[topk] + t2[0], t1[0] + t2[topk]))
        thr = 0.5 * (c[topk - 1] + c_next)
        z = jnp.zeros((1, tm), F32)
        for kk in range(topk):
            z = z + jnp.exp(c[kk] - c[0])
        rows = slice(h * nk, (h + 1) * nk)
        e2 = jnp.exp(s2 - t2[0])
        for lc in range(tm // LANES):
            cols = slice(lc * LANES, (lc + 1) * LANES)
            e2_ref[lc, rows, :] = e2[:, cols]
            s2_ref[lc, rows, :] = s2[:, cols]
        n_sub = thr_ref.shape[1] // PEER_HEADS
        thr_a = thr - s1
        c_a = 0.5 * jnp.exp(s1 - t1[0]) / z
        for blk in range(nk // n_sub):
            thr_ref[blk, h * n_sub:(h + 1) * n_sub, :] = thr_a[blk * n_sub:(blk + 1) * n_sub, :]
            c_ref[blk, h * n_sub:(h + 1) * n_sub, :] = c_a[blk * n_sub:(blk + 1) * n_sub, :]


def peer_route(q, k1, k2, n_sub, *, tm=512):
    t, d = q.shape
    rows = PEER_HEADS * PEER_N_KEYS
    blocks = PEER_N_KEYS // n_sub
    a_spec = pl.BlockSpec((blocks, PEER_HEADS * n_sub, tm), lambda i: (0, 0, i))
    a_shape = jax.ShapeDtypeStruct((blocks, PEER_HEADS * n_sub, t), F32)
    b_spec = pl.BlockSpec((tm // LANES, rows, LANES), lambda i: (i, 0, 0))
    b_shape = jax.ShapeDtypeStruct((t // LANES, rows, LANES), F32)
    return pl.pallas_call(
        _peer_route_kernel,
        grid=(t // tm,),
        in_specs=[
            pl.BlockSpec((tm, d), lambda i: (i, 0)),
            pl.BlockSpec((PEER_N_KEYS, PEER_HALF), lambda i: (0, 0)),
            pl.BlockSpec((PEER_N_KEYS, PEER_HALF), lambda i: (0, 0)),
        ],
        out_specs=[a_spec, a_spec, b_spec, b_spec],
        out_shape=[a_shape, a_shape, b_shape, b_shape],
        scratch_shapes=[pltpu.VMEM((_PEER_CAND_ROWS, tm), F32)],
        compiler_params=_params("parallel"),
    )(q, k1, k2)


def _twice_gelu_tanh(x):
    c = math.sqrt(2.0 / math.pi)
    t = jnp.tanh(x * (c + (c * 0.044715) * (x * x)))
    return x + x * t


def _peer_expert_kernel(xt_ref, u_ref, vt_ref, thr_ref, c_ref, e2_ref, s2_ref, res_ref, zero_ref, o_ref,
                        acc_ref, w0_ref, w1_ref, g_ref, *, n_blocks):
    e = pl.program_id(1)
    nk = PEER_N_KEYS
    eb, d = u_ref.shape
    tt = xt_ref.shape[1]
    n_sub = eb // nk
    lane_tiles = tt // LANES
    n_stages = eb // PEER_ACT_CHUNK
    keys_per_stage = PEER_ACT_CHUNK // nk
    out_per_stage = d // PEER_OUT_CHUNK // n_stages

    def gate_tiles(a):
        zeros = []
        sub = PEER_GATE_ROWS
        for lc in range(lane_tiles):
            cols = slice(lc * LANES, (lc + 1) * LANES)
            folded = None
            for b0 in range(0, nk, sub):
                gate = jnp.zeros((sub, LANES), F32)
                for h in range(PEER_HEADS):
                    r = h * n_sub + a
                    rows = slice(h * nk + b0, h * nk + b0 + sub)
                    gate = gate + jnp.where(s2_ref[lc, rows, :] >= thr_ref[0, r:r + 1, cols],
                                            e2_ref[lc, rows, :] * c_ref[0, r:r + 1, cols], 0.0)
                g_ref[a * nk + b0:a * nk + b0 + sub, cols] = gate
                for r in range(0, sub, 8):
                    folded = gate[r:r + 8, :] if folded is None else folded + gate[r:r + 8, :]
            zeros.append(pltpu.bitcast(folded, jnp.uint32) & zero_ref[...])
        return zeros

    def paced(lhs, zeros):
        if not zeros:
            return lhs
        z = functools.reduce(jnp.bitwise_or, zeros)
        words = pltpu.bitcast(lhs, jnp.uint32)
        words = words | jnp.tile(z, (words.shape[0] // z.shape[0], words.shape[1] // z.shape[1]))
        return pltpu.bitcast(words, lhs.dtype)

    def run(w_old_ref, w_new_ref):
        for s in range(n_stages):
            zeros = []
            if w_new_ref is not None:
                for a in range(s * keys_per_stage, (s + 1) * keys_per_stage):
                    zeros += gate_tiles(a)
            if w_old_ref is not None:
                for k in range(s * out_per_stage, (s + 1) * out_per_stage):
                    rows = slice(k * PEER_OUT_CHUNK, (k + 1) * PEER_OUT_CHUNK)
                    share = len(zeros) // out_per_stage
                    lhs = paced(vt_ref[rows, :], zeros[(k % out_per_stage) * share:(k % out_per_stage + 1) * share])
                    acc_ref[rows, :] += jnp.dot(lhs, w_old_ref[...], preferred_element_type=F32)
            if w_new_ref is not None:
                rows = slice(s * PEER_ACT_CHUNK, (s + 1) * PEER_ACT_CHUNK)
                act = _twice_gelu_tanh(jnp.dot(paced(u_ref[rows, :], zeros), xt_ref[...],
                                               preferred_element_type=F32))
                w_new_ref[rows, :] = (g_ref[rows, :] * act).astype(BF16)

    @pl.when(e == 0)
    def _():
        acc_ref[...] = jnp.zeros_like(acc_ref)
        run(None, w0_ref)

    is_mid = jnp.logical_and(e > 0, e < n_blocks)

    @pl.when(jnp.logical_and(is_mid, e % 2 == 1))
    def _():
        run(w0_ref, w1_ref)

    @pl.when(jnp.logical_and(is_mid, e % 2 == 0))
    def _():
        run(w1_ref, w0_ref)

    @pl.when(e == n_blocks)
    def _():
        run(w1_ref if n_blocks % 2 == 0 else w0_ref, None)
        o_ref[...] = res_ref[...] + acc_ref[...].T


def peer_experts(xt, u, vt, thr, c, e2, s2, res, *, tt=512):
    d, t = xt.shape
    n_exp = u.shape[0]
    eb = PEER_EXPERT_BLOCK
    n_blocks = n_exp // eb
    rows = PEER_HEADS * PEER_N_KEYS
    once = pl.Buffered(1)
    side_a = pl.BlockSpec((1, thr.shape[1], tt), lambda i, e: (jnp.minimum(e, n_blocks - 1), 0, i))
    side_b = pl.BlockSpec((tt // LANES, rows, LANES), lambda i, e: (i, 0, 0), pipeline_mode=once)
    kernel = functools.partial(_peer_expert_kernel, n_blocks=n_blocks)
    return pl.pallas_call(
        kernel,
        grid=(t // tt, n_blocks + 1),
        in_specs=[
            pl.BlockSpec((d, tt), lambda i, e: (0, i), pipeline_mode=once),
            pl.BlockSpec((eb, d), lambda i, e: (jnp.minimum(e, n_blocks - 1), 0)),
            pl.BlockSpec((d, eb), lambda i, e: (0, jnp.maximum(e - 1, 0))),
            side_a, side_a, side_b, side_b,
            pl.BlockSpec((tt, d), lambda i, e: (i, 0), pipeline_mode=once),
            pl.BlockSpec((8, LANES), lambda i, e: (0, 0)),
        ],
        out_specs=pl.BlockSpec((tt, d), lambda i, e: (i, 0)),
        out_shape=jax.ShapeDtypeStruct((t, d), F32),
        scratch_shapes=[pltpu.VMEM((d, tt), F32), pltpu.VMEM((eb, tt), BF16), pltpu.VMEM((eb, tt), BF16),
                        pltpu.VMEM((eb, tt), F32)],
        compiler_params=_params("parallel", "arbitrary"),
    )(xt, u, vt, thr, c, e2, s2, res, jnp.zeros((8, LANES), jnp.uint32))


def _transpose_cast_kernel(x_ref, o_ref):
    o_ref[...] = x_ref[0].T.astype(o_ref.dtype)


def transpose_cast(x, layer, dtype, *, tr=1024):
    _, r, c = x.shape
    return pl.pallas_call(
        _transpose_cast_kernel,
        grid=(r // tr,),
        in_specs=[pl.BlockSpec((1, tr, c), lambda i: (layer, i, 0))],
        out_specs=pl.BlockSpec((c, tr), lambda i: (0, i)),
        out_shape=jax.ShapeDtypeStruct((c, r), dtype),
        compiler_params=_params("parallel"),
    )(x)


def _cast_kernel(x_ref, o_ref):
    o_ref[...] = x_ref[0].astype(o_ref.dtype)


def layer_cast(x, layer, dtype, *, tr=1024):
    _, r, c = x.shape
    return pl.pallas_call(
        _cast_kernel,
        grid=(r // tr,),
        in_specs=[pl.BlockSpec((1, tr, c), lambda i: (layer, i, 0))],
        out_specs=pl.BlockSpec((tr, c), lambda i: (i, 0)),
        out_shape=jax.ShapeDtypeStruct((r, c), dtype),
        compiler_params=_params("parallel"),
    )(x)


def peer_mixer(x2d, ln, wq, k1, k2, u_all, v_all, layer):
    q, xt = norm_matmul(x2d, ln, wq.astype(BF16), BF16, tm=512, tn=wq.shape[1], with_transposed_norm=True)
    thr, c, e2, s2 = peer_route(q, k1.astype(BF16), k2.astype(BF16), PEER_EXPERT_BLOCK // PEER_N_KEYS)
    return peer_experts(xt, layer_cast(u_all, layer, BF16), transpose_cast(v_all, layer, BF16), thr, c, e2, s2, x2d)


def kernel(x, ln_mix, ln_ffn, mlstm_w_in, mlstm_gate_bias, mlstm_head_norm, mlstm_w_out,
           attn_w_in, attn_q_norm, attn_k_norm, attn_w_out,
           peer_wq, peer_k1, peer_k2, peer_u, peer_v):
    batch, seq, d_model = x.shape
    depth = ln_mix.shape[0]
    x2d = x.reshape(batch * seq, d_model)
    for layer in range(depth):
        j = layer // 2
        if layer % 2 == 0:
            x2d = mlstm_mixer(x2d, ln_mix[layer], mlstm_w_in[j], mlstm_gate_bias[j], mlstm_head_norm[j],
                              mlstm_w_out[j], batch, seq)
        else:
            x2d = attn_mixer(x2d, ln_mix[layer], attn_w_in[j], attn_q_norm[j], attn_k_norm[j],
                             attn_w_out[j], batch, seq)
        x2d = peer_mixer(x2d, ln_ffn[layer], peer_wq[layer], peer_k1[layer], peer_k2[layer],
                         peer_u, peer_v, layer)
    return x2d.reshape(batch, seq, d_model)
```

```python
import functools
import math

import jax
import jax.numpy as jnp
from jax import lax
from jax.experimental import pallas as pl
from jax.experimental.pallas import tpu as pltpu

NORM_EPS = 1e-6

MLSTM_HEADS = 4
MLSTM_QK_DIM = 256
MLSTM_V_DIM = 512
MLSTM_BLOCKS = 2

ATTN_HEAD_DIM = 128
ATTN_Q_HEADS = 16
ATTN_KV_HEADS = 4
ATTN_GROUPS = ATTN_Q_HEADS // ATTN_KV_HEADS
GRID_W = 64
ROPE_THETA = 10000.0

PEER_HEADS = 8
PEER_N_KEYS = 128
PEER_TOPK = 16
PEER_HALF = 128
PEER_EXPERT_BLOCK = 1024
PEER_OUT_CHUNK = 256
PEER_ACT_CHUNK = 256
PEER_GATE_ROWS = 32

LANES = 128
VMEM_LIMIT_BYTES = 56 * 1024 * 1024

BF16 = jnp.bfloat16
F32 = jnp.float32

_NT = (((1,), (1,)), ((), ()))


def _params(*sem):
    return pltpu.CompilerParams(dimension_semantics=sem, vmem_limit_bytes=VMEM_LIMIT_BYTES)


def _rms(x, g):
    return x * lax.rsqrt(jnp.mean(x * x, axis=-1, keepdims=True) + NORM_EPS) * g


def _norm_matmul_kernel(x_ref, g_ref, w_ref, o_ref, xn_ref):
    @pl.when(pl.program_id(1) == 0)
    def _():
        xn_ref[...] = _rms(x_ref[...], g_ref[...]).astype(BF16)

    o_ref[...] = jnp.dot(xn_ref[...], w_ref[...], preferred_element_type=F32).astype(o_ref.dtype)


def _norm_matmul_t_kernel(x_ref, g_ref, w_ref, o_ref, xt_ref):
    y = _rms(x_ref[...], g_ref[...])
    xt_ref[...] = y.T.astype(BF16)
    o_ref[...] = jnp.dot(y.astype(BF16), w_ref[...], preferred_element_type=F32).astype(o_ref.dtype)


def norm_matmul(x, g, w, out_dtype, *, tm=1024, tn=1024, with_transposed_norm=False):
    m, k = x.shape
    n = w.shape[1]
    tn = min(tn, n)
    grid = (m // tm, n // tn)
    in_specs = [
        pl.BlockSpec((tm, k), lambda i, j: (i, 0)),
        pl.BlockSpec((1, k), lambda i, j: (0, 0)),
        pl.BlockSpec((k, tn), lambda i, j: (0, j)),
    ]
    o_spec = pl.BlockSpec((tm, tn), lambda i, j: (i, j))
    scratch = [pltpu.VMEM((tm, k), BF16)]
    if with_transposed_norm:
        assert grid[1] == 1
        return pl.pallas_call(
            _norm_matmul_t_kernel,
            grid=grid, in_specs=in_specs,
            out_specs=[o_spec, pl.BlockSpec((k, tm), lambda i, j: (0, i))],
            out_shape=[jax.ShapeDtypeStruct((m, n), out_dtype), jax.ShapeDtypeStruct((k, m), BF16)],
            compiler_params=_params("parallel", "arbitrary"),
        )(x, g.reshape(1, k), w)
    return pl.pallas_call(
        _norm_matmul_kernel,
        grid=grid, in_specs=in_specs, out_specs=o_spec,
        out_shape=jax.ShapeDtypeStruct((m, n), out_dtype),
        scratch_shapes=scratch,
        compiler_params=_params("parallel", "arbitrary"),
    )(x, g.reshape(1, k), w)


def _matmul_residual_kernel(x_ref, w_ref, r_ref, o_ref):
    o_ref[...] = r_ref[...] + jnp.dot(x_ref[...], w_ref[...], preferred_element_type=F32)


def matmul_residual(x, w, res, *, tm=512, tn=2048):
    m, k = x.shape
    n = w.shape[1]
    return pl.pallas_call(
        _matmul_residual_kernel,
        grid=(m // tm, n // tn),
        in_specs=[
            pl.BlockSpec((tm, k), lambda i, j: (i, 0)),
            pl.BlockSpec((k, tn), lambda i, j: (0, j)),
            pl.BlockSpec((tm, tn), lambda i, j: (i, j)),
        ],
        out_specs=pl.BlockSpec((tm, tn), lambda i, j: (i, j)),
        out_shape=jax.ShapeDtypeStruct((m, n), F32),
        compiler_params=_params("parallel", "arbitrary"),
    )(x, w, res)


def _scan_lanes(x, op, fill, is_suffix):
    s = x.shape[-1]
    lane = lax.broadcasted_iota(jnp.int32, x.shape, 1)
    d = 1
    while d < s:
        from_left = jnp.where(lane >= d, pltpu.roll(x, d, axis=1), fill)
        from_right = jnp.where(lane < s - d, pltpu.roll(x, s - d, axis=1), fill)
        x = op(x, jnp.where(is_suffix, from_right, from_left))
        d *= 2
    return x


def _gate_scan_kernel(ig_ref, fg_ref, bi_ref, bf_ref, a_ref, m_ref, e_ref):
    ig = ig_ref[0] + bi_ref[...]
    fg = fg_ref[0] + bf_ref[...]
    is_suffix = lax.broadcasted_iota(jnp.int32, ig.shape, 0) >= MLSTM_HEADS
    logsig = jnp.minimum(fg, 0.0) - jnp.log1p(jnp.exp(-jnp.abs(fg)))
    fcum = _scan_lanes(logsig, jnp.add, 0.0, is_suffix)
    a = ig - fcum
    mx = jnp.maximum(_scan_lanes(a, jnp.maximum, -jnp.inf, is_suffix), 0.0)
    a_ref[0] = a
    m_ref[0] = mx
    e_ref[0] = jnp.exp(-(fcum + mx))


def gate_scan(ig, fg, bi, bf):
    b, r, s = ig.shape
    g_spec = pl.BlockSpec((1, r, s), lambda i: (i, 0, 0))
    b_spec = pl.BlockSpec((r, 1), lambda i: (0, 0))
    o_shape = jax.ShapeDtypeStruct((b, r, s), F32)
    return pl.pallas_call(
        _gate_scan_kernel,
        grid=(b,),
        in_specs=[g_spec, g_spec, b_spec, b_spec],
        out_specs=[g_spec] * 3,
        out_shape=[o_shape] * 3,
        compiler_params=_params("parallel"),
    )(ig, fg, bi, bf)


def _mlstm_kernel(q_ref, k_ref, v_ref, o_ref, row_ref, col_ref, hn_ref, out_ref, *, tq):
    i = pl.program_id(2)
    j = 1 - i
    other_is_past = i > j
    q = q_ref[...]
    col = col_ref[0, 0]
    m_f, m_b, e_f, e_b = col[:, 0:1], col[:, 1:2], col[:, 2:3], col[:, 3:4]
    scale = MLSTM_QK_DIM ** -0.5

    def scores(blk):
        start = pl.multiple_of(blk * tq, tq)
        kb = k_ref[pl.ds(start, tq), :]
        vb = v_ref[pl.ds(start, tq), :]
        return lax.dot_general(q, kb, _NT, preferred_element_type=F32) * scale, vb

    def weighted(p, vb):
        return jnp.dot(p.astype(BF16), vb, preferred_element_type=F32), jnp.sum(p, axis=-1, keepdims=True)

    s_o, v_o = scores(j)
    rows_o = row_ref[0, 0, j]
    a_o = jnp.where(other_is_past, rows_o[0:1, :], rows_o[1:2, :])
    m_o = jnp.where(other_is_past, m_f, m_b)
    num_o, den_o = weighted(s_o * jnp.exp(a_o - m_o), v_o)

    s_d, v_d = scores(i)
    rows_d = row_ref[0, 0, i]
    t_idx = lax.broadcasted_iota(jnp.int32, (tq, tq), 0)
    s_idx = lax.broadcasted_iota(jnp.int32, (tq, tq), 1)
    num_f, den_f = weighted(s_d * jnp.where(s_idx <= t_idx, jnp.exp(rows_d[0:1, :] - m_f), 0.0), v_d)
    num_b, den_b = weighted(s_d * jnp.where(s_idx >= t_idx, jnp.exp(rows_d[1:2, :] - m_b), 0.0), v_d)

    num_f = num_f + jnp.where(other_is_past, num_o, 0.0)
    den_f = den_f + jnp.where(other_is_past, den_o, 0.0)
    num_b = num_b + jnp.where(other_is_past, 0.0, num_o)
    den_b = den_b + jnp.where(other_is_past, 0.0, den_o)

    h = num_f / jnp.maximum(jnp.abs(den_f), e_f) + num_b / jnp.maximum(jnp.abs(den_b), e_b)
    h = _rms(h, hn_ref[0])
    out_ref[...] = (h * jax.nn.sigmoid(o_ref[...].astype(F32))).astype(out_ref.dtype)


def mlstm_core(proj, rowp, colp, head_norm, batch, seq):
    h, dk, dv = MLSTM_HEADS, MLSTM_QK_DIM, MLSTM_V_DIM
    nq = MLSTM_BLOCKS
    assert seq % nq == 0
    tq = seq // nq
    k_col0 = h * dk // dk
    v_col0 = 2 * h * dk // dv
    o_col0 = (2 * h * dk + h * dv) // dv
    kernel = functools.partial(_mlstm_kernel, tq=tq)
    return pl.pallas_call(
        kernel,
        grid=(batch, h, nq),
        in_specs=[
            pl.BlockSpec((tq, dk), lambda b, hh, i: (b * nq + i, hh)),
            pl.BlockSpec((seq, dk), lambda b, hh, i: (b, k_col0 + hh)),
            pl.BlockSpec((seq, dv), lambda b, hh, i: (b, v_col0 + hh)),
            pl.BlockSpec((tq, dv), lambda b, hh, i: (b * nq + i, o_col0 + hh)),
            pl.BlockSpec((1, 1, nq, 8, tq), lambda b, hh, i: (b, hh, 0, 0, 0)),
            pl.BlockSpec((1, 1, tq, 8), lambda b, hh, i: (b, hh, i, 0)),
            pl.BlockSpec((1, 1, dv), lambda b, hh, i: (hh, 0, 0)),
        ],
        out_specs=pl.BlockSpec((tq, dv), lambda b, hh, i: (b * nq + i, hh)),
        out_shape=jax.ShapeDtypeStruct((batch * seq, h * dv), BF16),
        compiler_params=_params("parallel", "parallel", "arbitrary"),
    )(proj, proj, proj, proj, rowp, colp, head_norm.reshape(h, 1, dv))


def mlstm_mixer(x2d, ln, w_in, gate_bias, head_norm, w_out, batch, seq):
    h = MLSTM_HEADS
    n_main = 2 * h * MLSTM_QK_DIM + 2 * h * MLSTM_V_DIM
    w_main = w_in[:, :n_main].astype(BF16)
    w_gate = jnp.pad(w_in[:, n_main:], ((0, 0), (0, LANES - 4 * h))).astype(BF16)
    proj = norm_matmul(x2d, ln, w_main, BF16, tn=2048)
    gates = norm_matmul(x2d, ln, w_gate, F32)[:, :4 * h]
    gates_t = gates.reshape(batch, seq, 4, h).transpose(0, 2, 3, 1)
    bias = gate_bias.astype(F32)
    a, mx, en = gate_scan(gates_t[:, 0::2].reshape(batch, 2 * h, seq), gates_t[:, 1::2].reshape(batch, 2 * h, seq),
                          bias[0::2].reshape(2 * h, 1), bias[1::2].reshape(2 * h, 1))
    nk = MLSTM_BLOCKS
    tq = seq // nk
    rowp = a.reshape(batch, 2, h, nk, tq).transpose(0, 2, 3, 1, 4)
    rowp = jnp.pad(rowp, ((0, 0), (0, 0), (0, 0), (0, 6), (0, 0)))
    colp = jnp.concatenate([mx.reshape(batch, 2, h, seq), en.reshape(batch, 2, h, seq)], axis=1)
    colp = jnp.pad(colp.transpose(0, 2, 3, 1), ((0, 0), (0, 0), (0, 0), (0, 4)))
    hg = mlstm_core(proj, rowp, colp, head_norm, batch, seq)
    return matmul_residual(hg, w_out.astype(BF16), x2d)


def _qk_proj_kernel(x_ref, g_ref, w_ref, gain_ref, mult_ref, cos_ref, sin_ref, o_ref):
    xn = _rms(x_ref[...], g_ref[...]).astype(BF16)
    acc = jnp.dot(xn, w_ref[...], preferred_element_type=F32)
    hd = ATTN_HEAD_DIM
    cos = cos_ref[...]
    sin = sin_ref[...]
    for c in range(acc.shape[1] // hd):
        sl = slice(c * hd, (c + 1) * hd)
        y = _rms(acc[:, sl], gain_ref[:, sl])
        y = y * cos + pltpu.roll(y, hd // 2, axis=1) * sin
        o_ref[:, sl] = (y * mult_ref[:, sl]).astype(o_ref.dtype)


def qk_proj(x, g, w, gain, mult, cos_t, sin_t, seq, *, tm=512):
    m, k = x.shape
    n = w.shape[1]
    nseq = seq // tm
    return pl.pallas_call(
        _qk_proj_kernel,
        grid=(m // tm,),
        in_specs=[
            pl.BlockSpec((tm, k), lambda i: (i, 0)),
            pl.BlockSpec((1, k), lambda i: (0, 0)),
            pl.BlockSpec((k, n), lambda i: (0, 0)),
            pl.BlockSpec((1, n), lambda i: (0, 0)),
            pl.BlockSpec((1, n), lambda i: (0, 0)),
            pl.BlockSpec((tm, ATTN_HEAD_DIM), lambda i: (i % nseq, 0)),
            pl.BlockSpec((tm, ATTN_HEAD_DIM), lambda i: (i % nseq, 0)),
        ],
        out_specs=pl.BlockSpec((tm, n), lambda i: (i, 0)),
        out_shape=jax.ShapeDtypeStruct((m, n), BF16),
        compiler_params=_params("parallel"),
    )(x, g.reshape(1, k), w, gain, mult, cos_t, sin_t)


def _attn_kernel(q_ref, k_ref, v_ref, o_ref):
    hd = ATTN_HEAD_DIM
    k = k_ref[...]
    v = v_ref[...]
    for g in range(ATTN_GROUPS):
        sl = slice(g * hd, (g + 1) * hd)
        s = lax.dot_general(q_ref[:, sl], k, _NT, preferred_element_type=F32)
        p = jnp.exp(s - jnp.max(s, axis=-1, keepdims=True))
        l = jnp.sum(p, axis=-1, keepdims=True)
        o = jnp.dot(p.astype(BF16), v, preferred_element_type=F32)
        o_ref[:, sl] = (o / l).astype(o_ref.dtype)


def attn_core(qk, v, batch, seq, *, tq=512):
    hd, g = ATTN_HEAD_DIM, ATTN_GROUPS
    nq = seq // tq
    return pl.pallas_call(
        _attn_kernel,
        grid=(batch, ATTN_KV_HEADS, nq),
        in_specs=[
            pl.BlockSpec((tq, g * hd), lambda b, kv, i: (b * nq + i, kv)),
            pl.BlockSpec((seq, hd), lambda b, kv, i: (b, ATTN_Q_HEADS + kv)),
            pl.BlockSpec((seq, hd), lambda b, kv, i: (b, kv)),
        ],
        out_specs=pl.BlockSpec((tq, g * hd), lambda b, kv, i: (b * nq + i, kv)),
        out_shape=jax.ShapeDtypeStruct((batch * seq, ATTN_Q_HEADS * hd), BF16),
        compiler_params=_params("parallel", "parallel", "arbitrary"),
    )(qk, qk, v)


def _rope_tables(seq):
    rows = seq // GRID_W
    row = jnp.repeat(jnp.arange(rows, dtype=F32), GRID_W)
    col = jnp.tile(jnp.arange(GRID_W, dtype=F32), rows)
    n_freq = ATTN_HEAD_DIM // 4
    freqs = ROPE_THETA ** (-jnp.arange(n_freq, dtype=F32) / n_freq)
    ang = jnp.concatenate([row[:, None] * freqs, col[:, None] * freqs], axis=-1)
    cos, sin = jnp.cos(ang), jnp.sin(ang)
    return jnp.concatenate([cos, cos], axis=-1), jnp.concatenate([-sin, sin], axis=-1)


def attn_mixer(x2d, ln, w_in, q_norm, k_norm, w_out, batch, seq):
    hd = ATTN_HEAD_DIM
    n_qk_heads = ATTN_Q_HEADS + ATTN_KV_HEADS
    n_qk = n_qk_heads * hd
    perm = jnp.concatenate([jnp.arange(0, hd, 2), jnp.arange(1, hd, 2)])
    d_model = w_in.shape[0]
    w_qk = w_in[:, :n_qk].reshape(d_model, n_qk_heads, hd)[:, :, perm].reshape(d_model, n_qk).astype(BF16)
    w_v = w_in[:, n_qk:].astype(BF16)
    gain = jnp.concatenate([jnp.tile(q_norm[perm], ATTN_Q_HEADS), jnp.tile(k_norm[perm], ATTN_KV_HEADS)])
    mult = jnp.concatenate([jnp.full((ATTN_Q_HEADS * hd,), hd ** -0.5, F32),
                            jnp.ones((ATTN_KV_HEADS * hd,), F32)])
    cos_t, sin_t = _rope_tables(seq)
    qk = qk_proj(x2d, ln, w_qk, gain.reshape(1, n_qk).astype(F32), mult.reshape(1, n_qk), cos_t, sin_t, seq)
    v = norm_matmul(x2d, ln, w_v, BF16)
    o = attn_core(qk, v, batch, seq)
    return matmul_residual(o, w_out.astype(BF16), x2d)


_PEER_CAND = [(i, j) for i in range(PEER_TOPK) for j in range(min(PEER_TOPK, (PEER_TOPK + 1) // (i + 1)))]
_PEER_CAND_ROWS = -(-len(_PEER_CAND) // 8) * 8


def _top_values(x, n):
    vals = []
    for _ in range(n):
        m = jnp.max(x, axis=0, keepdims=True)
        vals.append(m)
        x = jnp.where(x == m, -jnp.inf, x)
    return vals


def _sorting_network(n):
    pairs = []
    p = 1
    while p < n:
        k = p
        while k >= 1:
            for j in range(k % p, n - k, 2 * k):
                for i in range(min(k, n - j - k)):
                    if (i + j) // (2 * p) == (i + j + k) // (2 * p):
                        pairs.append((i + j, i + j + k))
            k //= 2
        p *= 2
    return pairs


def _top_values_of_keys(x, n):
    slabs = [x[r:r + 8, :] for r in range(0, x.shape[0], 8)]
    for i, j in _sorting_network(len(slabs)):
        slabs[i], slabs[j] = jnp.maximum(slabs[i], slabs[j]), jnp.minimum(slabs[i], slabs[j])
    vals = []
    for k in range(n):
        m = jnp.max(slabs[0], axis=0, keepdims=True)
        vals.append(m)
        hit = slabs[0] == m
        for r in range(min(n - 1 - k, len(slabs))):
            below = slabs[r + 1] if r + 1 < len(slabs) else -jnp.inf
            slabs[r] = jnp.where(hit, below, slabs[r])
    return vals


def _peer_route_kernel(q_ref, k1_ref, k2_ref, thr_ref, c_ref, e2_ref, s2_ref, cand_ref):
    nk, half, topk = PEER_N_KEYS, PEER_HALF, PEER_TOPK
    k1 = k1_ref[...]
    k2 = k2_ref[...]
    tm = q_ref.shape[0]
    cand_ref[...] = jnp.full(cand_ref.shape, -jnp.inf, F32)
    for h in range(PEER_HEADS):
        qa = q_ref[:, (2 * h) * half:(2 * h + 1) * half]
        qb = q_ref[:, (2 * h + 1) * half:(2 * h + 2) * half]
        s1 = lax.dot_general(k1, qa, _NT, preferred_element_type=F32)
        s2 = lax.dot_general(k2, qb, _NT, preferred_element_type=F32)
        t1 = _top_values_of_keys(s1, topk + 1)
        t2 = _top_values_of_keys(s2, topk + 1)
        for r, (i, j) in enumerate(_PEER_CAND):
            cand_ref[r:r + 1, :] = t1[i] + t2[j]
        c = _top_values(cand_ref[...], topk + 1)
        c_next = jnp.maximum(c[topk], jnp.maximum(t1[topk] + t2[0], t1[0] + t2[topk]))
        thr = 0.5 * (c[topk - 1] + c_next)
        z = jnp.zeros((1, tm), F32)
        for kk in range(topk):
            z = z + jnp.exp(c[kk] - c[0])
        rows = slice(h * nk, (h + 1) * nk)
        e2 = jnp.exp(s2 - t2[0])
        for lc in range(tm // LANES):
            cols = slice(lc * LANES, (lc + 1) * LANES)
            e2_ref[lc, rows, :] = e2[:, cols]
            s2_ref[lc, rows, :] = s2[:, cols]
        n_sub = thr_ref.shape[1] // PEER_HEADS
        thr_a = thr - s1
        c_a = 0.5 * jnp.exp(s1 - t1[0]) / z
        for blk in range(nk // n_sub):
            thr_ref[blk, h * n_sub:(h + 1) * n_sub, :] = thr_a[blk * n_sub:(blk + 1) * n_sub, :]
            c_ref[blk, h * n_sub:(h + 1) * n_sub, :] = c_a[blk * n_sub:(blk + 1) * n_sub, :]


def peer_route(q, k1, k2, n_sub, *, tm=512):
    t, d = q.shape
    rows = PEER_HEADS * PEER_N_KEYS
    blocks = PEER_N_KEYS // n_sub
    a_spec = pl.BlockSpec((blocks, PEER_HEADS * n_sub, tm), lambda i: (0, 0, i))
    a_shape = jax.ShapeDtypeStruct((blocks, PEER_HEADS * n_sub, t), F32)
    b_spec = pl.BlockSpec((tm // LANES, rows, LANES), lambda i: (i, 0, 0))
    b_shape = jax.ShapeDtypeStruct((t // LANES, rows, LANES), F32)
    return pl.pallas_call(
        _peer_route_kernel,
        grid=(t // tm,),
        in_specs=[
            pl.BlockSpec((tm, d), lambda i: (i, 0)),
            pl.BlockSpec((PEER_N_KEYS, PEER_HALF), lambda i: (0, 0)),
            pl.BlockSpec((PEER_N_KEYS, PEER_HALF), lambda i: (0, 0)),
        ],
        out_specs=[a_spec, a_spec, b_spec, b_spec],
        out_shape=[a_shape, a_shape, b_shape, b_shape],
        scratch_shapes=[pltpu.VMEM((_PEER_CAND_ROWS, tm), F32)],
        compiler_params=_params("parallel"),
    )(q, k1, k2)


def _twice_gelu_tanh(x):
    c = math.sqrt(2.0 / math.pi)
    t = jnp.tanh(x * (c + (c * 0.044715) * (x * x)))
    return x + x * t


def _peer_expert_kernel(xt_ref, u_ref, vt_ref, thr_ref, c_ref, e2_ref, s2_ref, res_ref, zero_ref, o_ref,
                        acc_ref, w0_ref, w1_ref, g_ref, *, n_blocks):
    e = pl.program_id(1)
    nk = PEER_N_KEYS
    eb, d = u_ref.shape
    tt = xt_ref.shape[1]
    n_sub = eb // nk
    lane_tiles = tt // LANES
    n_stages = eb // PEER_ACT_CHUNK
    keys_per_stage = PEER_ACT_CHUNK // nk
    out_per_stage = d // PEER_OUT_CHUNK // n_stages

    def gate_tiles(a):
        zeros = []
        sub = PEER_GATE_ROWS
        for lc in range(lane_tiles):
            cols = slice(lc * LANES, (lc + 1) * LANES)
            folded = None
            for b0 in range(0, nk, sub):
                gate = None
                for h in range(PEER_HEADS):
                    r = h * n_sub + a
                    rows = slice(h * nk + b0, h * nk + b0 + sub)
                    term = jnp.where(s2_ref[lc, rows, :] >= thr_ref[0, r:r + 1, cols],
                                     e2_ref[lc, rows, :] * c_ref[0, r:r + 1, cols], 0.0)
                    gate = term if gate is None else gate + term
                g_ref[a * nk + b0:a * nk + b0 + sub, cols] = gate
                for r in range(0, sub, 8):
                    folded = gate[r:r + 8, :] if folded is None else folded + gate[r:r + 8, :]
            zeros.append(pltpu.bitcast(folded, jnp.uint32) & zero_ref[...])
        return zeros

    def paced(lhs, zeros):
        if not zeros:
            return lhs
        z = functools.reduce(jnp.bitwise_or, zeros)
        words = pltpu.bitcast(lhs, jnp.uint32)
        words = words | jnp.tile(z, (words.shape[0] // z.shape[0], words.shape[1] // z.shape[1]))
        return pltpu.bitcast(words, lhs.dtype)

    def run(w_old_ref, w_new_ref):
        for s in range(n_stages):
            zeros = []
            if w_new_ref is not None:
                for a in range(s * keys_per_stage, (s + 1) * keys_per_stage):
                    zeros += gate_tiles(a)
            if w_old_ref is not None:
                for k in range(s * out_per_stage, (s + 1) * out_per_stage):
                    rows = slice(k * PEER_OUT_CHUNK, (k + 1) * PEER_OUT_CHUNK)
                    share = len(zeros) // out_per_stage
                    lhs = paced(vt_ref[rows, :], zeros[(k % out_per_stage) * share:(k % out_per_stage + 1) * share])
                    acc_ref[rows, :] += jnp.dot(lhs, w_old_ref[...], preferred_element_type=F32)
            if w_new_ref is not None:
                rows = slice(s * PEER_ACT_CHUNK, (s + 1) * PEER_ACT_CHUNK)
                act = _twice_gelu_tanh(jnp.dot(paced(u_ref[rows, :], zeros), xt_ref[...],
                                               preferred_element_type=F32))
                w_new_ref[rows, :] = (g_ref[rows, :] * act).astype(BF16)

    @pl.when(e == 0)
    def _():
        acc_ref[...] = jnp.zeros_like(acc_ref)
        run(None, w0_ref)

    is_mid = jnp.logical_and(e > 0, e < n_blocks)

    @pl.when(jnp.logical_and(is_mid, e % 2 == 1))
    def _():
        run(w0_ref, w1_ref)

    @pl.when(jnp.logical_and(is_mid, e % 2 == 0))
    def _():
        run(w1_ref, w0_ref)

    @pl.when(e == n_blocks)
    def _():
        run(w1_ref if n_blocks % 2 == 0 else w0_ref, None)
        o_ref[...] = res_ref[...] + acc_ref[...].T


def peer_experts(xt, u, vt, thr, c, e2, s2, res, *, tt=512):
    d, t = xt.shape
    n_exp = u.shape[0]
    eb = PEER_EXPERT_BLOCK
    n_blocks = n_exp // eb
    rows = PEER_HEADS * PEER_N_KEYS
    once = pl.Buffered(1)
    side_a = pl.BlockSpec((1, thr.shape[1], tt), lambda i, e: (jnp.minimum(e, n_blocks - 1), 0, i))
    side_b = pl.BlockSpec((tt // LANES, rows, LANES), lambda i, e: (i, 0, 0), pipeline_mode=once)
    kernel = functools.partial(_peer_expert_kernel, n_blocks=n_blocks)
    return pl.pallas_call(
        kernel,
        grid=(t // tt, n_blocks + 1),
        in_specs=[
            pl.BlockSpec((d, tt), lambda i, e: (0, i), pipeline_mode=once),
            pl.BlockSpec((eb, d), lambda i, e: (jnp.minimum(e, n_blocks - 1), 0)),
            pl.BlockSpec((d, eb), lambda i, e: (0, jnp.maximum(e - 1, 0))),
            side_a, side_a, side_b, side_b,
            pl.BlockSpec((tt, d), lambda i, e: (i, 0), pipeline_mode=once),
            pl.BlockSpec((8, LANES), lambda i, e: (0, 0)),
        ],
        out_specs=pl.BlockSpec((tt, d), lambda i, e: (i, 0)),
        out_shape=jax.ShapeDtypeStruct((t, d), F32),
        scratch_shapes=[pltpu.VMEM((d, tt), F32), pltpu.VMEM((eb, tt), BF16), pltpu.VMEM((eb, tt), BF16),
                        pltpu.VMEM((eb, tt), F32)],
        compiler_params=_params("parallel", "arbitrary"),
    )(xt, u, vt, thr, c, e2, s2, res, jnp.zeros((8, LANES), jnp.uint32))


def _transpose_cast_kernel(x_ref, o_ref):
    o_ref[...] = x_ref[0].T.astype(o_ref.dtype)


def transpose_cast(x, layer, dtype, *, tr=1024):
    _, r, c = x.shape
    return pl.pallas_call(
        _transpose_cast_kernel,
        grid=(r // tr,),
        in_specs=[pl.BlockSpec((1, tr, c), lambda i: (layer, i, 0))],
        out_specs=pl.BlockSpec((c, tr), lambda i: (0, i)),
        out_shape=jax.ShapeDtypeStruct((c, r), dtype),
        compiler_params=_params("parallel"),
    )(x)


def _cast_kernel(x_ref, o_ref):
    o_ref[...] = x_ref[0].astype(o_ref.dtype)


def layer_cast(x, layer, dtype, *, tr=1024):
    _, r, c = x.shape
    return pl.pallas_call(
        _cast_kernel,
        grid=(r // tr,),
        in_specs=[pl.BlockSpec((1, tr, c), lambda i: (layer, i, 0))],
        out_specs=pl.BlockSpec((tr, c), lambda i: (i, 0)),
        out_shape=jax.ShapeDtypeStruct((r, c), dtype),
        compiler_params=_params("parallel"),
    )(x)


def peer_mixer(x2d, ln, wq, k1, k2, u_all, v_all, layer):
    q, xt = norm_matmul(x2d, ln, wq.astype(BF16), BF16, tm=512, tn=wq.shape[1], with_transposed_norm=True)
    thr, c, e2, s2 = peer_route(q, k1.astype(BF16), k2.astype(BF16), PEER_EXPERT_BLOCK // PEER_N_KEYS)
    return peer_experts(xt, layer_cast(u_all, layer, BF16), transpose_cast(v_all, layer, BF16), thr, c, e2, s2, x2d)


def kernel(x, ln_mix, ln_ffn, mlstm_w_in, mlstm_gate_bias, mlstm_head_norm, mlstm_w_out,
           attn_w_in, attn_q_norm, attn_k_norm, attn_w_out,
           peer_wq, peer_k1, peer_k2, peer_u, peer_v):
    batch, seq, d_model = x.shape
    depth = ln_mix.shape[0]
    x2d = x.reshape(batch * seq, d_model)
    for layer in range(depth):
        j = layer // 2
        if layer % 2 == 0:
            x2d = mlstm_mixer(x2d, ln_mix[layer], mlstm_w_in[j], mlstm_gate_bias[j], mlstm_head_norm[j],
                              mlstm_w_out[j], batch, seq)
        else:
            x2d = attn_mixer(x2d, ln_mix[layer], attn_w_in[j], attn_q_norm[j], attn_k_norm[j],
                             attn_w_out[j], batch, seq)
        x2d = peer_mixer(x2d, ln_ffn[layer], peer_wq[layer], peer_k1[layer], peer_k2[layer],
                         peer_u, peer_v, layer)
    return x2d.reshape(batch, seq, d_model)
```
